```python
import math
import jax, jax.numpy as jnp
from jax import lax
import numpy as np

D_MODEL = 4096
BATCH = 1
SEQ = 8192
DEPTH = 2

CHUNK = 64
EPS = 1e-6
DN_HEADS = 16
DN_DK = 128
DN_DV = 128
DN_CONV = 4
HG_HEADS = 8
HG_DK = 128
HG_DV = 128
GLA_HEADS = 4
GLA_DK = 128
GLA_DV = 256
GLA_RANK = 16
GLA_GATE_NORM = 16.0
D_FF = 11008
N_BRANCH = 3

DN_QK = DN_HEADS * DN_DK
DN_V = DN_HEADS * DN_DV
DN_QKV = 2 * DN_QK + DN_V
HG_K = HG_HEADS * HG_DK
HG_V = HG_HEADS * HG_DV
GLA_K = GLA_HEADS * GLA_DK
GLA_V = GLA_HEADS * GLA_DV
MIX_WIDTH = DN_V + HG_V + GLA_V

IN_SIZES = (DN_QKV, DN_V, DN_HEADS, DN_HEADS,
            HG_K, HG_K, HG_V, HG_V,
            GLA_K, GLA_K, GLA_V, GLA_V, GLA_RANK,
            N_BRANCH * D_MODEL)
IN_COLS = sum(IN_SIZES)

kernel_name = "hybrid_deltanet_hgrn2_gla_macaron"


def rmsnorm(x, w):
    xf = x.astype(jnp.float32)
    y = xf * lax.rsqrt(jnp.mean(xf * xf, axis=-1, keepdims=True) + EPS)
    return (y * w.astype(jnp.float32)).astype(x.dtype)


def l2norm(x):
    return x * lax.rsqrt(jnp.sum(x * x, axis=-1, keepdims=True) + EPS)


def swiglu_ffn(x, w_in, w_out):
    g, u = jnp.split(x @ w_in, 2, axis=-1)
    return (jax.nn.silu(g) * u) @ w_out


def causal_depthwise_conv(x, w):
    width = w.shape[0]
    return lax.conv_general_dilated(
        x, w[:, None, :].astype(x.dtype), window_strides=(1,), padding=[(width - 1, 0)],
        dimension_numbers=("NWC", "WIO", "NWC"), feature_group_count=x.shape[-1])


def to_chunks(x, n_heads):
    b, t, _ = x.shape
    return x.reshape(b, t // CHUNK, CHUNK, n_heads, -1).transpose(0, 3, 1, 2, 4)


def to_chunks_scalar(x):
    b, t, h = x.shape
    return x.reshape(b, t // CHUNK, CHUNK, h).transpose(0, 3, 1, 2)


def from_chunks(o):
    b, h, n, c, d = o.shape
    return o.transpose(0, 2, 3, 1, 4).reshape(b, n * c, h, d)


def chunk_gla(q, k, v, log_g):
    b_, h_, n_, c_, dk = q.shape
    dv = v.shape[-1]
    bcum = jnp.cumsum(log_g, axis=3)
    causal = jnp.tril(jnp.ones((c_, c_), dtype=bool))

    def step(state, inp):
        qc, kc, vc, bc = inp
        diff = bc[:, :, :, None, :] - bc[:, :, None, :, :]
        decay = jnp.exp(jnp.where(causal[:, :, None], diff, -jnp.inf))
        scores = jnp.einsum("bhtd,bhsd,bhtsd->bhts", qc, kc, decay)
        o = (jnp.einsum("bhts,bhsv->bhtv", scores, vc)
             + jnp.einsum("bhtd,bhdv->bhtv", qc * jnp.exp(bc), state))
        b_last = bc[:, :, -1:, :]
        state = (state * jnp.exp(b_last[:, :, 0, :, None])
                 + jnp.einsum("bhsd,bhsv->bhdv", kc * jnp.exp(b_last - bc), vc))
        return state, o

    xs = tuple(jnp.moveaxis(a, 2, 0) for a in (q, k, v, bcum))
    s0 = jnp.zeros((b_, h_, dk, dv), jnp.float32)
    _, o = lax.scan(step, s0, xs)
    return jnp.moveaxis(o, 0, 2)


def chunk_gated_delta(q, k, v, beta, log_a):
    b_, h_, n_, c_, dk = q.shape
    dv = v.shape[-1]
    bcum = jnp.cumsum(log_a, axis=-1)
    causal = jnp.tril(jnp.ones((c_, c_), dtype=bool))
    strict = jnp.tril(jnp.ones((c_, c_), dtype=bool), k=-1)
    decay = jnp.exp(jnp.where(causal, bcum[..., :, None] - bcum[..., None, :], -jnp.inf))
    k_beta = k * beta[..., None]
    kk = jnp.einsum("bhntd,bhnsd->bhnts", k_beta, k) * decay
    lower = jnp.eye(c_, dtype=jnp.float32) + jnp.where(strict, kk, 0.0)
    rhs = jnp.concatenate([v * beta[..., None], k_beta * jnp.exp(bcum)[..., None]], axis=-1)
    sol = lax.linalg.triangular_solve(lower, rhs, left_side=True, lower=True, unit_diagonal=True)
    u, w = sol[..., :dv], sol[..., dv:]
    qk = jnp.einsum("bhntd,bhnsd->bhnts", q, k) * decay
    q_dec = q * jnp.exp(bcum)[..., None]
    k_dec = k * jnp.exp(bcum[..., -1:] - bcum)[..., None]
    a_last = jnp.exp(bcum[..., -1])

    def step(state, inp):
        u_c, w_c, qk_c, qd_c, kd_c, al_c = inp
        v_new = u_c - jnp.einsum("bhtd,bhdv->bhtv", w_c, state)
        o = (jnp.einsum("bhtd,bhdv->bhtv", qd_c, state)
             + jnp.einsum("bhts,bhsv->bhtv", qk_c, v_new))
        state = state * al_c[..., None, None] + jnp.einsum("bhsd,bhsv->bhdv", kd_c, v_new)
        return state, o

    xs = tuple(jnp.moveaxis(a, 2, 0) for a in (u, w, qk, q_dec, k_dec, a_last))
    s0 = jnp.zeros((b_, h_, dk, dv), jnp.float32)
    _, o = lax.scan(step, s0, xs)
    return jnp.moveaxis(o, 0, 2)


def hybrid_mixer(u, w_in, dn_conv, dn_a_log, dn_dt_bias, dn_norm, hg_lb, hg_norm,
                 gla_w_gk2, gla_b_gk, gla_norm, w_branch, w_out):
    bsz, t, _ = u.shape
    f32 = lambda a: a.astype(jnp.float32)
    proj = u @ w_in
    splits = np.cumsum(IN_SIZES)[:-1].tolist()
    (dn_qkv, dn_z, dn_b, dn_al, hg_q, hg_f, hg_i, hg_g,
     gl_q, gl_k, gl_v, gl_g, gl_gk, gate_logits) = jnp.split(proj, splits, axis=-1)

    dn_qkv = jax.nn.silu(causal_depthwise_conv(dn_qkv, dn_conv))
    dq, dk, dv = jnp.split(f32(dn_qkv), [DN_QK, 2 * DN_QK], axis=-1)
    dq = l2norm(to_chunks(dq, DN_HEADS)) * (DN_DK ** -0.5)
    dk = l2norm(to_chunks(dk, DN_HEADS))
    dv = to_chunks(dv, DN_HEADS)
    beta = to_chunks_scalar(jax.nn.sigmoid(f32(dn_b)))
    log_a = to_chunks_scalar(-jnp.exp(f32(dn_a_log)) * jax.nn.softplus(f32(dn_al) + f32(dn_dt_bias)))
    o = from_chunks(chunk_gated_delta(dq, dk, dv, beta, log_a))
    z = f32(dn_z).reshape(bsz, t, DN_HEADS, DN_DV)
    o_dn = (rmsnorm(o, dn_norm) * jax.nn.silu(z)).reshape(bsz, t, DN_V).astype(u.dtype)

    xf = f32(hg_f)
    log_f = jnp.logaddexp(jnp.log(hg_lb), jnp.log1p(-hg_lb) + jax.nn.log_sigmoid(xf))
    k_in = (1.0 - hg_lb) * jax.nn.sigmoid(-xf)
    o = from_chunks(chunk_gla(to_chunks(f32(hg_q), HG_HEADS) * (HG_DK ** -0.5),
                              to_chunks(k_in, HG_HEADS),
                              to_chunks(f32(hg_i), HG_HEADS),
                              to_chunks(log_f, HG_HEADS)))
    g = f32(hg_g).reshape(bsz, t, HG_HEADS, HG_DV)
    o_hg = (rmsnorm(o, hg_norm) * jax.nn.silu(g)).reshape(bsz, t, HG_V).astype(u.dtype)

    gk = f32(gl_gk) @ f32(gla_w_gk2) + f32(gla_b_gk)
    log_g = jax.nn.log_sigmoid(gk) / GLA_GATE_NORM
    o = from_chunks(chunk_gla(to_chunks(f32(gl_q), GLA_HEADS) * (GLA_DK ** -0.5),
                              to_chunks(f32(gl_k), GLA_HEADS),
                              to_chunks(f32(gl_v), GLA_HEADS),
                              to_chunks(log_g, GLA_HEADS)))
    g = f32(gl_g).reshape(bsz, t, GLA_HEADS, GLA_DV)
    o_gl = (rmsnorm(o, gla_norm) * jax.nn.silu(g)).reshape(bsz, t, GLA_V).astype(u.dtype)

    gates = jax.nn.sigmoid(gate_logits.reshape(bsz, t, N_BRANCH, D_MODEL))
    w_dn, w_hg, w_gl = jnp.split(w_branch, [DN_V, DN_V + HG_V], axis=0)
    y = (gates[:, :, 0] * (o_dn @ w_dn)
         + gates[:, :, 1] * (o_hg @ w_hg)
         + gates[:, :, 2] * (o_gl @ w_gl))
    return y @ w_out


def setup_inputs(seed: int = 0) -> dict:
    key = jax.random.key(seed)
    ks = jax.random.split(key, 24)
    L = DEPTH
    nrm = lambda k, shape, scale: jax.random.normal(k, shape, jnp.float32) * scale
    gain = lambda k, shape: 1.0 + 0.02 * jax.random.normal(k, shape, jnp.float32)
    dt = jnp.exp(jax.random.uniform(ks[8], (L, DN_HEADS), jnp.float32)
                 * (math.log(0.1) - math.log(1e-3)) + math.log(1e-3))
    kb = jax.random.split(ks[15], 3)
    w_branch = jnp.concatenate([
        nrm(kb[0], (L, DN_V, D_MODEL), DN_V ** -0.5),
        nrm(kb[1], (L, HG_V, D_MODEL), HG_V ** -0.5),
        nrm(kb[2], (L, GLA_V, D_MODEL), GLA_V ** -0.5)], axis=1)
    return {
        "x": nrm(ks[0], (BATCH, SEQ, D_MODEL), 1.0),
        "norm_ffn1": gain(ks[1], (L, D_MODEL)),
        "ffn1_w_in": nrm(ks[2], (L, D_MODEL, 2 * D_FF), D_MODEL ** -0.5),
        "ffn1_w_out": nrm(ks[3], (L, D_FF, D_MODEL), D_FF ** -0.5),
        "norm_mix": gain(ks[4], (L, D_MODEL)),
        "w_in": nrm(ks[5], (L, D_MODEL, IN_COLS), D_MODEL ** -0.5),
        "dn_conv": nrm(ks[6], (L, DN_CONV, DN_QKV), DN_CONV ** -0.5),
        "dn_a_log": jnp.log(jax.random.uniform(ks[7], (L, DN_HEADS), jnp.float32, 1.0, 16.0)),
        "dn_dt_bias": dt + jnp.log(-jnp.expm1(-dt)),
        "dn_norm": gain(ks[9], (L, DN_DV)),
        "hg_lower_bounds": nrm(ks[10], (L, HG_K), 1.0),
        "hg_norm": gain(ks[11], (L, HG_DV)),
        "gla_w_gk2": nrm(ks[12], (L, GLA_RANK, GLA_K), GLA_RANK ** -0.5),
        "gla_b_gk": nrm(ks[13], (L, GLA_K), 0.1),
        "gla_norm": gain(ks[14], (L, GLA_DV)),
        "w_branch": w_branch,
        "w_out": nrm(ks[16], (L, D_MODEL, D_MODEL), D_MODEL ** -0.5),
        "norm_ffn2": gain(ks[17], (L, D_MODEL)),
        "ffn2_w_in": nrm(ks[18], (L, D_MODEL, 2 * D_FF), D_MODEL ** -0.5),
        "ffn2_w_out": nrm(ks[19], (L, D_FF, D_MODEL), D_FF ** -0.5),
        "norm_final": gain(ks[20], (D_MODEL,)),
    }


def reference(x, norm_ffn1, ffn1_w_in, ffn1_w_out, norm_mix, w_in, dn_conv, dn_a_log,
              dn_dt_bias, dn_norm, hg_lower_bounds, hg_norm, gla_w_gk2, gla_b_gk, gla_norm,
              w_branch, w_out, norm_ffn2, ffn2_w_in, ffn2_w_out, norm_final):
    lb_all = jnp.cumsum(jax.nn.softmax(hg_lower_bounds.astype(jnp.float32), axis=0), axis=0)
    lb_all = lb_all - lb_all[0]
    h = x
    for l in range(DEPTH):
        h = h + 0.5 * swiglu_ffn(rmsnorm(h, norm_ffn1[l]), ffn1_w_in[l], ffn1_w_out[l])
        h = h + hybrid_mixer(rmsnorm(h, norm_mix[l]), w_in[l], dn_conv[l], dn_a_log[l],
                             dn_dt_bias[l], dn_norm[l], lb_all[l], hg_norm[l],
                             gla_w_gk2[l], gla_b_gk[l], gla_norm[l], w_branch[l], w_out[l])
        h = h + 0.5 * swiglu_ffn(rmsnorm(h, norm_ffn2[l]), ffn2_w_in[l], ffn2_w_out[l])
    return rmsnorm(h, norm_final)
```

```python
import functools

import jax
import jax.numpy as jnp
from jax import lax
from jax.experimental import pallas as pl
from jax.experimental.pallas import tpu as pltpu

D_MODEL = 4096
DEPTH = 2
CHUNK = 64
SUB = 16
EPS = 1e-6
DN_HEADS, DN_DK, DN_DV, DN_CONV = 16, 128, 128, 4
HG_HEADS, HG_DK, HG_DV = 8, 128, 128
GLA_HEADS, GLA_DK, GLA_DV, GLA_RANK = 4, 128, 256, 16
GLA_GATE_NORM = 16.0
D_FF = 11008
D_FF_PAD = 11264
N_BRANCH = 3

DN_QK = DN_HEADS * DN_DK
DN_V = DN_HEADS * DN_DV
HG_K = HG_HEADS * HG_DK
HG_V = HG_HEADS * HG_DV
GLA_K = GLA_HEADS * GLA_DK
GLA_V = GLA_HEADS * GLA_DV

OFF_DN_Q, OFF_DN_K, OFF_DN_V, OFF_DN_Z = 0, 2048, 4096, 6144
OFF_HG_Q, OFF_HG_F, OFF_HG_I, OFF_HG_G = 8192, 9216, 10240, 11264
OFF_GL_Q, OFF_GL_K, OFF_GL_V, OFF_GL_G = 12288, 12800, 13312, 14336
OFF_GATES = 15360
MAIN_COLS = OFF_GATES + N_BRANCH * D_MODEL
DN_HB = 4
DN_GROUPS = DN_HEADS // DN_HB
LANES = 128
ALPHA_LANE = 16
SMALL_COLS = (DN_GROUPS + 1) * LANES

VMEM_LIMIT = 56 * 1024 * 1024
HI = lax.Precision.HIGHEST


def _cp(sem):
    return pltpu.CompilerParams(dimension_semantics=sem, vmem_limit_bytes=VMEM_LIMIT)


def _dot(a, b, precision=None):
    return jnp.dot(a, b, preferred_element_type=jnp.float32, precision=precision)


def _dot_nt(a, b, precision=None):
    return lax.dot_general(a, b, (((1,), (1,)), ((), ())),
                           preferred_element_type=jnp.float32, precision=precision)


def _dot_tn(a, b, precision=None):
    return lax.dot_general(a, b, (((0,), (0,)), ((), ())),
                           preferred_element_type=jnp.float32, precision=precision)


def _sigmoid(x):
    return jax.nn.sigmoid(x)


def _softplus(x):
    return jnp.maximum(x, 0.0) + jnp.log1p(jnp.exp(-jnp.abs(x)))


def _log_sigmoid(x):
    return jnp.minimum(x, 0.0) - jnp.log1p(jnp.exp(-jnp.abs(x)))


def _rmsnorm_kernel(x_ref, w_ref, o_ref):
    x = x_ref[...]
    y = x * lax.rsqrt(jnp.mean(x * x, axis=-1, keepdims=True) + EPS)
    o_ref[...] = (y * w_ref[...]).astype(o_ref.dtype)


def rmsnorm(x, w, out_dtype, tr=256):
    t, d = x.shape
    return pl.pallas_call(
        _rmsnorm_kernel,
        grid=(t // tr,),
        in_specs=[pl.BlockSpec((tr, d), lambda i: (i, 0)),
                  pl.BlockSpec((1, d), lambda i: (0, 0))],
        out_specs=pl.BlockSpec((tr, d), lambda i: (i, 0)),
        out_shape=jax.ShapeDtypeStruct((t, d), out_dtype),
        compiler_params=_cp(("parallel",)),
        name="rmsnorm",
    )(x, w.reshape(1, d))


def _mm_kernel(a_ref, w_ref, o_ref):
    o_ref[...] = _dot(a_ref[...], w_ref[...]).astype(o_ref.dtype)


def matmul(a, w, out_dtype, tm, tn):
    m, k = a.shape
    n = w.shape[1]
    return pl.pallas_call(
        _mm_kernel,
        grid=(m // tm, n // tn),
        in_specs=[pl.BlockSpec((tm, k), lambda i, j: (i, 0)),
                  pl.BlockSpec((k, tn), lambda i, j: (0, j))],
        out_specs=pl.BlockSpec((tm, tn), lambda i, j: (i, j)),
        out_shape=jax.ShapeDtypeStruct((m, n), out_dtype),
        compiler_params=_cp(("parallel", "arbitrary")),
        name="matmul",
    )(a, w)


def _swiglu_kernel(a_ref, w_ref, o_ref, *, tn):
    r = _dot(a_ref[...], w_ref[...])
    g = r[:, :tn]
    u = r[:, tn:]
    o_ref[...] = (g * _sigmoid(g) * u).astype(o_ref.dtype)


def swiglu_in(a, w_gu, tm, tn):
    m, k = a.shape
    nf = w_gu.shape[1] // 2
    return pl.pallas_call(
        functools.partial(_swiglu_kernel, tn=tn),
        grid=(m // tm, nf // tn),
        in_specs=[pl.BlockSpec((tm, k), lambda i, j: (i, 0)),
                  pl.BlockSpec((k, 2 * tn), lambda i, j: (0, j))],
        out_specs=pl.BlockSpec((tm, tn), lambda i, j: (i, j)),
        out_shape=jax.ShapeDtypeStruct((m, nf), jnp.bfloat16),
        compiler_params=_cp(("parallel", "arbitrary")),
        name="swiglu_in",
    )(a, w_gu)


def _mm_res_kernel(a_ref, w_ref, r_ref, o_ref, acc_ref, *, scale, nk):
    kk = pl.program_id(2)
    p = _dot(a_ref[...], w_ref[...])

    @pl.when(kk == 0)
    def _():
        acc_ref[...] = p

    @pl.when(kk > 0)
    def _():
        acc_ref[...] += p

    @pl.when(kk == nk - 1)
    def _():
        o_ref[...] = r_ref[...] + scale * acc_ref[...]


def matmul_residual(a, w, res, scale, tm, tn, tk):
    m, k = a.shape
    n = w.shape[1]
    nk = k // tk
    return pl.pallas_call(
        functools.partial(_mm_res_kernel, scale=scale, nk=nk),
        grid=(m // tm, n // tn, nk),
        in_specs=[pl.BlockSpec((tm, tk), lambda i, j, kk: (i, kk)),
                  pl.BlockSpec((tk, tn), lambda i, j, kk: (kk, j)),
                  pl.BlockSpec((tm, tn), lambda i, j, kk: (i, j))],
        out_specs=pl.BlockSpec((tm, tn), lambda i, j, kk: (i, j)),
        out_shape=jax.ShapeDtypeStruct((m, n), jnp.float32),
        scratch_shapes=[pltpu.VMEM((tm, tn), jnp.float32)],
        compiler_params=_cp(("parallel", "parallel", "arbitrary")),
        name="matmul_residual",
    )(a, w, res)


def _merge_kernel(a0_ref, a1_ref, a2_ref, w0_ref, w1_ref, w2_ref,
                  g0_ref, g1_ref, g2_ref, o_ref):
    y = _sigmoid(g0_ref[...]) * _dot(a0_ref[...], w0_ref[...])
    y = y + _sigmoid(g1_ref[...]) * _dot(a1_ref[...], w1_ref[...])
    y = y + _sigmoid(g2_ref[...]) * _dot(a2_ref[...], w2_ref[...])
    o_ref[...] = y.astype(o_ref.dtype)


def branch_merge(o_dn, o_hg, o_gl, w_branch, proj, tm, tn):
    m = o_dn.shape[0]
    n = D_MODEL
    gate_blk = OFF_GATES // tn
    per_branch = D_MODEL // tn
    hg_blk = DN_V // HG_V
    gl_blk = (DN_V + HG_V) // GLA_V

    def gate_spec(b):
        return pl.BlockSpec((tm, tn), lambda i, j: (i, gate_blk + b * per_branch + j))

    return pl.pallas_call(
        _merge_kernel,
        grid=(m // tm, n // tn),
        in_specs=[pl.BlockSpec((tm, DN_V), lambda i, j: (i, 0)),
                  pl.BlockSpec((tm, HG_V), lambda i, j: (i, 0)),
                  pl.BlockSpec((tm, GLA_V), lambda i, j: (i, 0)),
                  pl.BlockSpec((DN_V, tn), lambda i, j: (0, j)),
                  pl.BlockSpec((HG_V, tn), lambda i, j: (hg_blk, j)),
                  pl.BlockSpec((GLA_V, tn), lambda i, j: (gl_blk, j)),
                  gate_spec(0), gate_spec(1), gate_spec(2)],
        out_specs=pl.BlockSpec((tm, tn), lambda i, j: (i, j)),
        out_shape=jax.ShapeDtypeStruct((m, n), jnp.bfloat16),
        compiler_params=_cp(("parallel", "arbitrary")),
        name="branch_merge",
    )(o_dn, o_hg, o_gl, w_branch, w_branch, w_branch, proj, proj, proj)


def _iota2(shape, dim):
    return lax.broadcasted_iota(jnp.int32, shape, dim)


def _chunk_cumsum_matrix(tb, lower):
    r = _iota2((tb, tb), 0)
    c = _iota2((tb, tb), 1)
    same = (r // CHUNK) == (c // CHUNK)
    tri = (c <= r) if lower else (r <= c)
    return jnp.where(same & tri, 1.0, 0.0).astype(jnp.float32)


def _head_out(o, nw, gate):
    y = o * lax.rsqrt(jnp.mean(o * o, axis=-1, keepdims=True) + EPS) * nw
    return y * (gate * _sigmoid(gate))


def _unit_lower_inverse(a_strict):
    r = _iota2((CHUNK, CHUNK), 0)
    c = _iota2((CHUNK, CHUNK), 1)
    eye = jnp.where(r == c, 1.0, 0.0).astype(jnp.float32)
    blk = (r // SUB) == (c // SUB)
    d = jnp.where(blk, a_strict, 0.0)
    e = a_strict - d
    d2 = _dot(d, d, HI)
    d4 = _dot(d2, d2, HI)
    d8 = _dot(d4, d4, HI)
    p = eye - d
    p = p + _dot(p, d2, HI)
    p = p + _dot(p, d4, HI)
    p = p + _dot(p, d8, HI)
    n = _dot(p, e, HI)
    n2 = _dot(n, n, HI)
    q = eye - n + n2 - _dot(n, n2, HI)
    return q, p


def _dn_kernel(q_ref, k_ref, v_ref, z_ref, sm_ref, cq_ref, ck_ref, cv_ref,
               alog_ref, dtb_ref, nw_ref, o_ref, s_ref, carry_ref, *, tb):
    t = pl.program_id(1)

    @pl.when(t == 0)
    def _():
        s_ref[...] = jnp.zeros_like(s_ref)
        carry_ref[...] = jnp.zeros_like(carry_ref)

    nc = tb // CHUNK

    def conv_silu(x_ref, cw_ref, idx):
        x = x_ref[...]
        ext = jnp.concatenate([carry_ref[idx], x], axis=0)
        w = cw_ref[...]
        y = ext[8:] * w[DN_CONV - 1:DN_CONV]
        for s in range(1, DN_CONV):
            y = y + pltpu.roll(ext, s, 0)[8:] * w[DN_CONV - 1 - s:DN_CONV - s]
        carry_ref[idx] = x[tb - 8:]
        return y * _sigmoid(y)

    yq = conv_silu(q_ref, cq_ref, 0)
    yk = conv_silu(k_ref, ck_ref, 1)
    yv = conv_silu(v_ref, cv_ref, 2)

    sm = sm_ref[...]
    lane = _iota2((1, LANES), 1)
    is_alpha = (lane >= ALPHA_LANE) & (lane < ALPHA_LANE + DN_HB)
    a_coef = jnp.where(is_alpha, jnp.exp(alog_ref[0]), 0.0)
    beta_slab = _sigmoid(sm)
    la_slab = -a_coef * _softplus(sm + dtb_ref[0])
    bcum_slab = _dot(_chunk_cumsum_matrix(tb, True), la_slab, HI)
    la_t = la_slab.T[ALPHA_LANE:ALPHA_LANE + 8]
    bcum_t = _dot(la_t, _chunk_cumsum_matrix(tb, False), HI)

    r64 = _iota2((CHUNK, CHUNK), 0)
    c64 = _iota2((CHUNK, CHUNK), 1)
    causal = c64 <= r64
    strict = c64 < r64
    nw = nw_ref[...]
    z_all = z_ref[...]

    for h in range(DN_HB):
        hs = slice(h * DN_DK, (h + 1) * DN_DK)
        qh, kh, vh = yq[:, hs], yk[:, hs], yv[:, hs]
        qn = qh * lax.rsqrt(jnp.sum(qh * qh, axis=-1, keepdims=True) + EPS) * (DN_DK ** -0.5)
        kn = kh * lax.rsqrt(jnp.sum(kh * kh, axis=-1, keepdims=True) + EPS)
        beta = beta_slab[:, h:h + 1]
        bc = bcum_slab[:, ALPHA_LANE + h:ALPHA_LANE + h + 1]
        br = bcum_t[h:h + 1, :]
        ebc = jnp.exp(bc)
        kb = kn * beta
        vb = vh * beta
        kbe = kb * ebc
        qd = qn * ebc
        zh = z_all[:, hs]
        for c in range(nc):
            rs = slice(c * CHUNK, (c + 1) * CHUNK)
            bc_c = bc[rs]
            diff = bc_c - br[:, rs]
            decay = jnp.where(causal, jnp.exp(jnp.where(causal, diff, 0.0)), 0.0)
            k_c = kn[rs]
            kk = _dot_nt(kb[rs], k_c, HI) * decay
            qmat, pmat = _unit_lower_inverse(jnp.where(strict, kk, 0.0))
            rhs = jnp.concatenate([vb[rs], kbe[rs]], axis=-1)
            sol = _dot(qmat, _dot(pmat, rhs, HI), HI)
            u_c = sol[:, :DN_DV]
            w_c = sol[:, DN_DV:]
            qk = _dot_nt(qn[rs], k_c, HI) * decay
            b_last = bc_c[CHUNK - 1:CHUNK]
            kd = k_c * jnp.exp(b_last - bc_c)
            state = s_ref[h]
            v_new = u_c - _dot(w_c, state, HI)
            o = _dot(qd[rs], state, HI) + _dot(qk, v_new, HI)
            s_ref[h] = state * jnp.exp(b_last) + _dot_tn(kd, v_new, HI)
            o_ref[rs, hs] = _head_out(o, nw, zh[rs]).astype(o_ref.dtype)


def deltanet(proj, small, dn_conv, a_log_pad, dtb_pad, dn_norm, tb):
    t = proj.shape[0]
    w = DN_HB * DN_DK
    qb, kb_, vb, zb = (OFF_DN_Q // w, OFF_DN_K // w, OFF_DN_V // w, OFF_DN_Z // w)

    def pspec(base):
        return pl.BlockSpec((tb, w), lambda g, i: (i, base + g))

    def cspec(base):
        return pl.BlockSpec((DN_CONV, w), lambda g, i: (0, base + g))

    pad_spec = pl.BlockSpec((1, 1, LANES), lambda g, i: (g, 0, 0))
    return pl.pallas_call(
        functools.partial(_dn_kernel, tb=tb),
        grid=(DN_GROUPS, t // tb),
        in_specs=[pspec(qb), pspec(kb_), pspec(vb), pspec(zb),
                  pl.BlockSpec((tb, LANES), lambda g, i: (i, g)),
                  cspec(qb), cspec(kb_), cspec(vb),
                  pad_spec, pad_spec,
                  pl.BlockSpec((1, DN_DV), lambda g, i: (0, 0))],
        out_specs=pl.BlockSpec((tb, w), lambda g, i: (i, g)),
        out_shape=jax.ShapeDtypeStruct((t, DN_V), jnp.bfloat16),
        scratch_shapes=[pltpu.VMEM((DN_HB, DN_DK, DN_DV), jnp.float32),
                        pltpu.VMEM((3, 8, w), jnp.float32)],
        compiler_params=_cp(("parallel", "arbitrary")),
        name="deltanet",
    )(proj, proj, proj, proj, small, dn_conv, dn_conv, dn_conv,
      a_log_pad, dtb_pad, dn_norm.reshape(1, DN_DV))


def _gla_core(q, k, v, g, gate, nw, st_ref, o_ref, tb):
    dk = q.shape[1]
    nsub = tb // SUB
    nc = tb // CHUNK
    per = CHUNK // SUB
    b = _dot(_chunk_cumsum_matrix(tb, True), g, HI)

    b3 = b.reshape(nsub, SUB, dk)
    last = b3[:, SUB - 1:SUB, :]
    prev = jnp.concatenate([jnp.zeros((1, 1, dk), jnp.float32), last[:nsub - 1]], axis=0)
    blk_id = _iota2((nsub, 1, dk), 0)
    base3 = jnp.where(blk_id % per == 0, 0.0, prev)
    base = jnp.broadcast_to(base3, (nsub, SUB, dk)).reshape(tb, dk)
    qt = q * jnp.exp(b - base)

    q3 = q.reshape(nsub, SUB, dk)
    k3 = k.reshape(nsub, SUB, dk)
    row = _iota2((tb, CHUNK), 0)
    col = _iota2((tb, CHUNK), 1)
    same_blk = ((row % CHUNK) // SUB) == (col // SUB)
    diag = jnp.zeros((tb, CHUNK), jnp.float32)
    for j in range(SUB):
        bj = b3[:, j:j + 1, :]
        kj = k3[:, j:j + 1, :]
        e = jnp.exp(jnp.minimum(b3 - bj, 0.0))
        colv = jnp.sum((q3 * kj * e).reshape(tb, dk), axis=-1, keepdims=True)
        sel = same_blk & ((col % SUB) == j) & ((row % SUB) >= j)
        diag = jnp.where(sel, colv, diag)

    r64 = _iota2((CHUNK, CHUNK), 0)
    c64 = _iota2((CHUNK, CHUNK), 1)
    qe = q * jnp.exp(b)
    for c in range(nc):
        rs = slice(c * CHUNK, (c + 1) * CHUNK)
        b_c = b[rs]
        k_c = k[rs]
        v_c = v[rs]
        scores = diag[rs]
        for i in range(1, per):
            n_i = b_c[i * SUB - 1:i * SUB]
            kt = k_c * jnp.exp(jnp.minimum(n_i - b_c, 0.0))
            r = _dot_nt(qt[rs], kt, HI)
            scores = scores + jnp.where(((r64 // SUB) == i) & (c64 < i * SUB), r, 0.0)
        b_last = b_c[CHUNK - 1:CHUNK]
        st = st_ref[...]
        o = _dot(scores, v_c, HI) + _dot_nt(qe[rs], st, HI)
        kg = k_c * jnp.exp(b_last - b_c)
        st_ref[...] = st * jnp.exp(b_last) + _dot_tn(v_c, kg, HI)
        o_ref[rs, :] = _head_out(o, nw, gate[rs]).astype(o_ref.dtype)


def _hg_kernel(q_ref, f_ref, i_ref, g_ref, lb_ref, nw_ref, o_ref, st_ref, *, tb):
    @pl.when(pl.program_id(1) == 0)
    def _():
        st_ref[...] = jnp.zeros_like(st_ref)

    lb = lb_ref[...]
    xf = f_ref[...]
    a = jnp.log(lb)
    cc = jnp.log1p(-lb) + _log_sigmoid(xf)
    log_f = jnp.maximum(a, cc) + jnp.log1p(jnp.exp(-jnp.abs(a - cc)))
    k_in = (1.0 - lb) * _sigmoid(-xf)
    q = q_ref[...] * (HG_DK ** -0.5)
    _gla_core(q, k_in, i_ref[...], log_f, g_ref[...], nw_ref[...], st_ref, o_ref, tb)


def hgrn2(proj, lb, hg_norm, tb):
    t = proj.shape[0]

    def pspec(off):
        return pl.BlockSpec((tb, HG_DK), lambda h, i: (i, off // HG_DK + h))

    return pl.pallas_call(
        functools.partial(_hg_kernel, tb=tb),
        grid=(HG_HEADS, t // tb),
        in_specs=[pspec(OFF_HG_Q), pspec(OFF_HG_F), pspec(OFF_HG_I), pspec(OFF_HG_G),
                  pl.BlockSpec((1, HG_DK), lambda h, i: (0, h)),
                  pl.BlockSpec((1, HG_DV), lambda h, i: (0, 0))],
        out_specs=pl.BlockSpec((tb, HG_DV), lambda h, i: (i, h)),
        out_shape=jax.ShapeDtypeStruct((t, HG_V), jnp.bfloat16),
        scratch_shapes=[pltpu.VMEM((HG_DV, HG_DK), jnp.float32)],
        compiler_params=_cp(("parallel", "arbitrary")),
        name="hgrn2",
    )(proj, proj, proj, proj, lb.reshape(1, HG_K), hg_norm.reshape(1, HG_DV))


def _gl_kernel(q_ref, k_ref, v_ref, g_ref, sm_ref, w2_ref, bgk_ref, nw_ref, o_ref, st_ref, *, tb):
    @pl.when(pl.program_id(1) == 0)
    def _():
        st_ref[...] = jnp.zeros_like(st_ref)

    gk = _dot(sm_ref[...], w2_ref[...], HI) + bgk_ref[...]
    log_g = _log_sigmoid(gk) / GLA_GATE_NORM
    q = q_ref[...] * (GLA_DK ** -0.5)
    _gla_core(q, k_ref[...], v_ref[...], log_g, g_ref[...], nw_ref[...], st_ref, o_ref, tb)


def gla(proj, small, w2_pad, b_gk, gla_norm, tb):
    t = proj.shape[0]
    return pl.pallas_call(
        functools.partial(_gl_kernel, tb=tb),
        grid=(GLA_HEADS, t // tb),
        in_specs=[pl.BlockSpec((tb, GLA_DK), lambda h, i: (i, OFF_GL_Q // GLA_DK + h)),
                  pl.BlockSpec((tb, GLA_DK), lambda h, i: (i, OFF_GL_K // GLA_DK + h)),
                  pl.BlockSpec((tb, GLA_DV), lambda h, i: (i, OFF_GL_V // GLA_DV + h)),
                  pl.BlockSpec((tb, GLA_DV), lambda h, i: (i, OFF_GL_G // GLA_DV + h)),
                  pl.BlockSpec((tb, LANES), lambda h, i: (i, DN_GROUPS)),
                  pl.BlockSpec((LANES, GLA_DK), lambda h, i: (0, h)),
                  pl.BlockSpec((1, GLA_DK), lambda h, i: (0, h)),
                  pl.BlockSpec((1, GLA_DV), lambda h, i: (0, 0))],
        out_specs=pl.BlockSpec((tb, GLA_DV), lambda h, i: (i, h)),
        out_shape=jax.ShapeDtypeStruct((t, GLA_V), jnp.bfloat16),
        scratch_shapes=[pltpu.VMEM((GLA_DV, GLA_DK), jnp.float32)],
        compiler_params=_cp(("parallel", "arbitrary")),
        name="gla",
    )(proj, proj, proj, proj, small, w2_pad, b_gk.reshape(1, GLA_K), gla_norm.reshape(1, GLA_DV))


FFN_TN = 512


def _prep_ffn(w_in, w_out):
    pad = D_FF_PAD - D_FF
    wg = jnp.pad(w_in[:, :D_FF], ((0, 0), (0, pad))).astype(jnp.bfloat16)
    wu = jnp.pad(w_in[:, D_FF:], ((0, 0), (0, pad))).astype(jnp.bfloat16)
    nt = D_FF_PAD // FFN_TN
    w_gu = jnp.stack([wg.reshape(D_MODEL, nt, FFN_TN), wu.reshape(D_MODEL, nt, FFN_TN)], axis=2)
    w_gu = w_gu.reshape(D_MODEL, 2 * D_FF_PAD)
    w_o = jnp.pad(w_out, ((0, pad), (0, 0))).astype(jnp.bfloat16)
    return w_gu, w_o


def _prep_mixer_in(w_in):
    o_b = DN_QK * 2 + DN_V + DN_V
    o_al = o_b + DN_HEADS
    o_hg = o_al + DN_HEADS
    o_gk = o_hg + 2 * HG_K + 2 * HG_V + 2 * GLA_K + 2 * GLA_V
    o_gate = o_gk + GLA_RANK
    main = jnp.concatenate([w_in[:, :o_b], w_in[:, o_hg:o_gk], w_in[:, o_gate:]], axis=1)
    d = w_in.shape[0]
    parts = []
    for g in range(DN_GROUPS):
        parts += [w_in[:, o_b + g * DN_HB:o_b + (g + 1) * DN_HB],
                  jnp.zeros((d, ALPHA_LANE - DN_HB), w_in.dtype),
                  w_in[:, o_al + g * DN_HB:o_al + (g + 1) * DN_HB],
                  jnp.zeros((d, LANES - ALPHA_LANE - DN_HB), w_in.dtype)]
    parts += [w_in[:, o_gk:o_gate], jnp.zeros((d, LANES - GLA_RANK), w_in.dtype)]
    small = jnp.concatenate(parts, axis=1)
    return main.astype(jnp.bfloat16), small.astype(jnp.bfloat16)


def _pad_group_lanes(v):
    vg = v.reshape(DN_GROUPS, 1, DN_HB).astype(jnp.float32)
    return jnp.pad(vg, ((0, 0), (0, 0), (ALPHA_LANE, LANES - ALPHA_LANE - DN_HB)))


def _ffn(h, norm_w, w_in, w_out):
    w_gu, w_o = _prep_ffn(w_in, w_out)
    u = rmsnorm(h, norm_w, jnp.bfloat16)
    a = swiglu_in(u, w_gu, tm=1024, tn=FFN_TN)
    return matmul_residual(a, w_o, h, 0.5, tm=1024, tn=1024, tk=D_FF_PAD // 4)


def _mixer(h, norm_w, w_in, dn_conv, dn_a_log, dn_dt_bias, dn_norm, lb, hg_norm,
           gla_w_gk2, gla_b_gk, gla_norm, w_branch, w_out, tb):
    w_main, w_small = _prep_mixer_in(w_in)
    u = rmsnorm(h, norm_w, jnp.bfloat16)
    proj = matmul(u, w_main, jnp.float32, tm=1024, tn=1024)
    small = matmul(u, w_small, jnp.float32, tm=1024, tn=SMALL_COLS)
    o_dn = deltanet(proj, small, dn_conv, _pad_group_lanes(dn_a_log), _pad_group_lanes(dn_dt_bias),
                    dn_norm, tb)
    o_hg = hgrn2(proj, lb, hg_norm, tb)
    w2_pad = jnp.pad(gla_w_gk2.astype(jnp.float32), ((0, LANES - GLA_RANK), (0, 0)))
    o_gl = gla(proj, small, w2_pad, gla_b_gk, gla_norm, tb)
    y = branch_merge(o_dn, o_hg, o_gl, w_branch.astype(jnp.bfloat16), proj, tm=1024, tn=512)
    return matmul_residual(y, w_out.astype(jnp.bfloat16), h, 1.0, tm=1024, tn=1024, tk=D_MODEL)


def kernel(x, norm_ffn1, ffn1_w_in, ffn1_w_out, norm_mix, w_in, dn_conv, dn_a_log, dn_dt_bias,
           dn_norm, hg_lower_bounds, hg_norm, gla_w_gk2, gla_b_gk, gla_norm, w_branch, w_out,
           norm_ffn2, ffn2_w_in, ffn2_w_out, norm_final):
    bsz, t, d = x.shape
    lb_all = jnp.cumsum(jax.nn.softmax(hg_lower_bounds.astype(jnp.float32), axis=0), axis=0)
    lb_all = lb_all - lb_all[0]
    h = x.reshape(bsz * t, d)
    tb = min(256, t)
    for l in range(DEPTH):
        h = _ffn(h, norm_ffn1[l], ffn1_w_in[l], ffn1_w_out[l])
        h = _mixer(h, norm_mix[l], w_in[l], dn_conv[l], dn_a_log[l], dn_dt_bias[l], dn_norm[l],
                   lb_all[l], hg_norm[l], gla_w_gk2[l], gla_b_gk[l], gla_norm[l],
                   w_branch[l], w_out[l], tb)
        h = _ffn(h, norm_ffn2[l], ffn2_w_in[l], ffn2_w_out[l])
    return rmsnorm(h, norm_final, x.dtype).reshape(bsz, t, d)
```

```python
import functools

import jax
import jax.numpy as jnp
from jax import lax
from jax.experimental import pallas as pl
from jax.experimental.pallas import tpu as pltpu

D_MODEL = 4096
DEPTH = 2
CHUNK = 64
SUB = 16
EPS = 1e-6
DN_HEADS, DN_DK, DN_DV, DN_CONV = 16, 128, 128, 4
HG_HEADS, HG_DK, HG_DV = 8, 128, 128
GLA_HEADS, GLA_DK, GLA_DV, GLA_RANK = 4, 128, 256, 16
GLA_GATE_NORM = 16.0
D_FF = 11008
N_BRANCH = 3

DN_QK = DN_HEADS * DN_DK
DN_V = DN_HEADS * DN_DV
HG_K = HG_HEADS * HG_DK
HG_V = HG_HEADS * HG_DV
GLA_K = GLA_HEADS * GLA_DK
GLA_V = GLA_HEADS * GLA_DV

SRC_DN_B = 2 * DN_QK + 2 * DN_V
SRC_HG = SRC_DN_B + 2 * DN_HEADS
SRC_GK = SRC_HG + 2 * HG_K + 2 * HG_V + 2 * GLA_K + 2 * GLA_V
SRC_GATES = SRC_GK + GLA_RANK
A_DN_Q, A_DN_K, A_DN_V, A_DN_Z = 0, 2048, 4096, 6144
B_HG_Q, B_HG_F, B_HG_I, B_HG_G = 0, 1024, 2048, 3072
B_GL_Q, B_GL_K, B_GL_V, B_GL_G = 4096, 4608, 5120, 6144
B_COLS = SRC_GK - SRC_HG

DN_HB = 8
DN_GROUPS = DN_HEADS // DN_HB
HG_HB = 4
GLA_HB = 4
DN_TB = 128
GLA_TB = 256
LANES = 128
ALPHA_LANE = 16
SMALL_COLS = (DN_GROUPS + 1) * LANES

VMEM_LIMIT = 56 * 1024 * 1024
HI = lax.Precision.HIGHEST
BF16 = jnp.bfloat16


def _cp(sem):
    return pltpu.CompilerParams(dimension_semantics=sem, vmem_limit_bytes=VMEM_LIMIT)


def _bf(x):
    return x if x.dtype == BF16 else x.astype(BF16)


def _dot(a, b, precision=None):
    return jnp.dot(a, b, preferred_element_type=jnp.float32, precision=precision)


def _bdot(a, b):
    return jnp.dot(_bf(a), _bf(b), preferred_element_type=jnp.float32)


def _bmm(a, b):
    return jnp.einsum("bmk,bkn->bmn", _bf(a), _bf(b), preferred_element_type=jnp.float32)


def _bmm_nt(a, b):
    return jnp.einsum("bmk,bnk->bmn", _bf(a), _bf(b), preferred_element_type=jnp.float32)


def _sigmoid(x):
    return jax.nn.sigmoid(x)


def _softplus(x):
    return jnp.maximum(x, 0.0) + jnp.log1p(jnp.exp(-jnp.abs(x)))


def _log_sigmoid(x):
    return jnp.minimum(x, 0.0) - jnp.log1p(jnp.exp(-jnp.abs(x)))


def _resident(shape, index_map):
    return pl.BlockSpec(shape, index_map, pipeline_mode=pl.Buffered(1))


def _rmsnorm_kernel(x_ref, w_ref, o_ref):
    x = x_ref[...]
    y = x * lax.rsqrt(jnp.mean(x * x, axis=-1, keepdims=True) + EPS)
    o_ref[...] = (y * w_ref[...]).astype(o_ref.dtype)


def rmsnorm(x, w, out_dtype, tr=256):
    t, d = x.shape
    return pl.pallas_call(
        _rmsnorm_kernel,
        grid=(t // tr,),
        in_specs=[pl.BlockSpec((tr, d), lambda i: (i, 0)),
                  pl.BlockSpec((1, d), lambda i: (0, 0))],
        out_specs=pl.BlockSpec((tr, d), lambda i: (i, 0)),
        out_shape=jax.ShapeDtypeStruct((t, d), out_dtype),
        compiler_params=_cp(("parallel",)),
        name="rmsnorm",
    )(x, w.reshape(1, d))


def _mm_kernel(a_ref, w_ref, o_ref):
    o_ref[...] = _bdot(a_ref[...], w_ref[...]).astype(o_ref.dtype)


def matmul(a, w, n, out_dtype, tm, tn):
    m, k = a.shape
    return pl.pallas_call(
        _mm_kernel,
        grid=(m // tm, n // tn),
        in_specs=[_resident((tm, k), lambda i, j: (i, 0)),
                  pl.BlockSpec((k, tn), lambda i, j: (0, j))],
        out_specs=pl.BlockSpec((tm, tn), lambda i, j: (i, j)),
        out_shape=jax.ShapeDtypeStruct((m, n), out_dtype),
        compiler_params=_cp(("parallel", "arbitrary")),
        name="matmul",
    )(a, w)


def _swiglu_kernel(a_ref, wg_ref, wu_ref, o_ref, *, tn):
    w = jnp.concatenate([_bf(wg_ref[...]), _bf(wu_ref[...])], axis=1)
    r = _bdot(a_ref[...], w)
    g = r[:, :tn]
    u = r[:, tn:]
    o_ref[...] = (g * _sigmoid(g) * u).astype(o_ref.dtype)


def swiglu_in(a, w_in, tm, tn):
    m, k = a.shape
    nf = w_in.shape[1] // 2
    nj = nf // tn
    return pl.pallas_call(
        functools.partial(_swiglu_kernel, tn=tn),
        grid=(m // tm, nj),
        in_specs=[_resident((tm, k), lambda i, j: (i, 0)),
                  pl.BlockSpec((k, tn), lambda i, j: (0, j)),
                  pl.BlockSpec((k, tn), lambda i, j: (0, nj + j))],
        out_specs=pl.BlockSpec((tm, tn), lambda i, j: (i, j)),
        out_shape=jax.ShapeDtypeStruct((m, nf), BF16),
        compiler_params=_cp(("parallel", "arbitrary")),
        name="swiglu_in",
    )(a, w_in, w_in)


def _mm_res_kernel(a_ref, w_ref, r_ref, o_ref, *, scale):
    o_ref[...] = r_ref[...] + scale * _bdot(a_ref[...], w_ref[...])


def matmul_residual(a, w, res, scale, tm, tn):
    m, k = a.shape
    n = w.shape[1]
    return pl.pallas_call(
        functools.partial(_mm_res_kernel, scale=scale),
        grid=(m // tm, n // tn),
        in_specs=[_resident((tm, k), lambda i, j: (i, 0)),
                  pl.BlockSpec((k, tn), lambda i, j: (0, j)),
                  pl.BlockSpec((tm, tn), lambda i, j: (i, j))],
        out_specs=pl.BlockSpec((tm, tn), lambda i, j: (i, j)),
        out_shape=jax.ShapeDtypeStruct((m, n), jnp.float32),
        compiler_params=_cp(("parallel", "arbitrary")),
        name="matmul_residual",
    )(a, w, res)


def _merge_kernel(a0_ref, a1_ref, a2_ref, w0_ref, w1_ref, w2_ref,
                  g0_ref, g1_ref, g2_ref, o_ref):
    y = _sigmoid(g0_ref[...]) * _bdot(a0_ref[...], w0_ref[...])
    y = y + _sigmoid(g1_ref[...]) * _bdot(a1_ref[...], w1_ref[...])
    y = y + _sigmoid(g2_ref[...]) * _bdot(a2_ref[...], w2_ref[...])
    o_ref[...] = y.astype(o_ref.dtype)


def branch_merge(o_dn, o_hg, o_gl, w_branch, gates, tm, tn):
    m = o_dn.shape[0]
    n = D_MODEL
    per_branch = D_MODEL // tn
    hg_blk = DN_V // HG_V
    gl_blk = (DN_V + HG_V) // GLA_V

    def gate_spec(b):
        return pl.BlockSpec((tm, tn), lambda i, j: (i, b * per_branch + j))

    return pl.pallas_call(
        _merge_kernel,
        grid=(m // tm, n // tn),
        in_specs=[_resident((tm, DN_V), lambda i, j: (i, 0)),
                  _resident((tm, HG_V), lambda i, j: (i, 0)),
                  _resident((tm, GLA_V), lambda i, j: (i, 0)),
                  pl.BlockSpec((DN_V, tn), lambda i, j: (0, j)),
                  pl.BlockSpec((HG_V, tn), lambda i, j: (hg_blk, j)),
                  pl.BlockSpec((GLA_V, tn), lambda i, j: (gl_blk, j)),
                  gate_spec(0), gate_spec(1), gate_spec(2)],
        out_specs=pl.BlockSpec((tm, tn), lambda i, j: (i, j)),
        out_shape=jax.ShapeDtypeStruct((m, n), BF16),
        compiler_params=_cp(("parallel", "arbitrary")),
        name="branch_merge",
    )(o_dn, o_hg, o_gl, w_branch, w_branch, w_branch, gates, gates, gates)


def _iota2(shape, dim):
    return lax.broadcasted_iota(jnp.int32, shape, dim)


def _chunk_cumsum_matrix(tb, lower):
    r = _iota2((tb, tb), 0)
    c = _iota2((tb, tb), 1)
    same = (r // CHUNK) == (c // CHUNK)
    tri = (c <= r) if lower else (r <= c)
    return jnp.where(same & tri, 1.0, 0.0).astype(jnp.float32)


def _head_out(o, nw, gate):
    y = o * lax.rsqrt(jnp.mean(o * o, axis=-1, keepdims=True) + EPS) * nw
    return y * (gate * _sigmoid(gate))


def _chunks(x, nc):
    return [x[c * CHUNK:(c + 1) * CHUNK] for c in range(nc)]


def _unit_lower_inverse(a_strict):
    r = _iota2((CHUNK, CHUNK), 0)
    c = _iota2((CHUNK, CHUNK), 1)
    eye = jnp.where(r == c, 1.0, 0.0).astype(jnp.float32)
    blk = (r // SUB) == (c // SUB)
    d = jnp.where(blk, a_strict, 0.0)
    e = a_strict - d
    d2 = _bmm(d, d)
    d4 = _bmm(d2, d2)
    d8 = _bmm(d4, d4)
    p = eye - d
    p = p + _bmm(p, d2)
    p = p + _bmm(p, d4)
    p = p + _bmm(p, d8)
    n = _bmm(p, e)
    n2 = _bmm(n, n)
    q = eye - n + n2 - _bmm(n, n2)
    return q, p


def _dn_kernel(q_ref, k_ref, v_ref, z_ref, sm_ref, cq_ref, ck_ref, cv_ref,
               alog_ref, dtb_ref, nw_ref, o_ref, s_ref, carry_ref, *, tb):
    t = pl.program_id(1)

    @pl.when(t == 0)
    def _():
        s_ref[...] = jnp.zeros_like(s_ref)
        carry_ref[...] = jnp.zeros_like(carry_ref)

    nc = tb // CHUNK
    hb = DN_HB

    def conv_silu(x_ref, cw_ref, idx):
        x = x_ref[...]
        ext = jnp.concatenate([carry_ref[idx], x], axis=0)
        w = cw_ref[...]
        y = ext[8:] * w[DN_CONV - 1:DN_CONV]
        for s in range(1, DN_CONV):
            y = y + pltpu.roll(ext, s, 0)[8:] * w[DN_CONV - 1 - s:DN_CONV - s]
        carry_ref[idx] = x[tb - 8:]
        return y * _sigmoid(y)

    yq = conv_silu(q_ref, cq_ref, 0)
    yk = conv_silu(k_ref, ck_ref, 1)
    yv = conv_silu(v_ref, cv_ref, 2)

    sm = sm_ref[...]
    lane = _iota2((1, LANES), 1)
    is_alpha = (lane >= ALPHA_LANE) & (lane < ALPHA_LANE + hb)
    a_coef = jnp.where(is_alpha, jnp.exp(alog_ref[0]), 0.0)
    beta_slab = _sigmoid(sm)
    la_slab = -a_coef * _softplus(sm + dtb_ref[0])
    bcum_slab = _dot(_chunk_cumsum_matrix(tb, True), la_slab, HI)
    la_t = la_slab.T[ALPHA_LANE:ALPHA_LANE + 8]
    bcum_t = _dot(la_t, _chunk_cumsum_matrix(tb, False), HI)

    def stack_heads(per_head):
        return jnp.stack([per_head[h][c] for c in range(nc) for h in range(hb)], axis=0)

    qs, ks, vs, betas, bcs = [], [], [], [], []
    for h in range(hb):
        hs = slice(h * DN_DK, (h + 1) * DN_DK)
        qh, kh = yq[:, hs], yk[:, hs]
        qn = qh * lax.rsqrt(jnp.sum(qh * qh, axis=-1, keepdims=True) + EPS) * (DN_DK ** -0.5)
        kn = kh * lax.rsqrt(jnp.sum(kh * kh, axis=-1, keepdims=True) + EPS)
        qs.append(_chunks(qn, nc))
        ks.append(_chunks(kn, nc))
        vs.append(_chunks(yv[:, hs], nc))
        betas.append(_chunks(beta_slab[:, h:h + 1], nc))
        bcs.append(_chunks(bcum_slab[:, ALPHA_LANE + h:ALPHA_LANE + h + 1], nc))
    q3, k3, v3 = stack_heads(qs), stack_heads(ks), stack_heads(vs)
    beta3, bc3 = stack_heads(betas), stack_heads(bcs)
    br3 = jnp.stack([bcum_t[h:h + 1, c * CHUNK:(c + 1) * CHUNK]
                     for c in range(nc) for h in range(hb)], axis=0)

    r64 = _iota2((CHUNK, CHUNK), 0)
    c64 = _iota2((CHUNK, CHUNK), 1)
    causal = c64 <= r64
    strict = c64 < r64
    ebc3 = jnp.exp(bc3)
    kb3 = k3 * beta3
    decay3 = jnp.where(causal, jnp.exp(jnp.where(causal, bc3 - br3, 0.0)), 0.0)
    kq3 = _bmm_nt(jnp.concatenate([kb3, q3], axis=1), k3)
    kk3 = kq3[:, :CHUNK] * decay3
    qk3 = kq3[:, CHUNK:] * decay3
    qmat, pmat = _unit_lower_inverse(jnp.where(strict, kk3, 0.0))
    rhs3 = jnp.concatenate([v3 * beta3, kb3 * ebc3], axis=2)
    sol3 = _bmm(qmat, _bmm(pmat, rhs3))
    u3 = sol3[:, :, :DN_DV]
    wq3 = jnp.concatenate([sol3[:, :, DN_DV:], q3 * ebc3], axis=1)
    b_last3 = bc3[:, CHUNK - 1:CHUNK, :]
    kd3 = k3 * jnp.exp(b_last3 - bc3)
    kdt3 = jnp.stack([kd3[b].T for b in range(nc * hb)], axis=0)
    a_last3 = jnp.exp(b_last3)

    nw = nw_ref[...]
    z_all = z_ref[...]
    state = s_ref[...]
    for c in range(nc):
        bs = slice(c * hb, (c + 1) * hb)
        rs = slice(c * CHUNK, (c + 1) * CHUNK)
        ws = _bmm(wq3[bs], state)
        v_new = u3[bs] - ws[:, :CHUNK]
        o = ws[:, CHUNK:] + _bmm(qk3[bs], v_new)
        state = state * a_last3[bs] + _bmm(kdt3[bs], v_new)
        for h in range(hb):
            hs = slice(h * DN_DK, (h + 1) * DN_DK)
            o_ref[rs, hs] = _head_out(o[h], nw, z_all[rs, hs]).astype(o_ref.dtype)
    s_ref[...] = state


def deltanet(proj_a, small, dn_conv, a_log_pad, dtb_pad, dn_norm, tb):
    t = proj_a.shape[0]
    w = DN_HB * DN_DK
    qb, kb_, vb, zb = (A_DN_Q // w, A_DN_K // w, A_DN_V // w, A_DN_Z // w)

    def pspec(base):
        return pl.BlockSpec((tb, w), lambda g, i: (i, base + g))

    def cspec(base):
        return pl.BlockSpec((DN_CONV, w), lambda g, i: (0, base + g))

    pad_spec = pl.BlockSpec((1, 1, LANES), lambda g, i: (g, 0, 0))
    return pl.pallas_call(
        functools.partial(_dn_kernel, tb=tb),
        grid=(DN_GROUPS, t // tb),
        in_specs=[pspec(qb), pspec(kb_), pspec(vb), pspec(zb),
                  pl.BlockSpec((tb, LANES), lambda g, i: (i, g)),
                  cspec(qb), cspec(kb_), cspec(vb),
                  pad_spec, pad_spec,
                  pl.BlockSpec((1, DN_DV), lambda g, i: (0, 0))],
        out_specs=pl.BlockSpec((tb, w), lambda g, i: (i, g)),
        out_shape=jax.ShapeDtypeStruct((t, DN_V), BF16),
        scratch_shapes=[pltpu.VMEM((DN_HB, DN_DK, DN_DV), jnp.float32),
                        pltpu.VMEM((3, 8, w), jnp.float32)],
        compiler_params=_cp(("parallel", "arbitrary")),
        name="deltanet",
    )(proj_a, proj_a, proj_a, proj_a, small, dn_conv, dn_conv, dn_conv,
      a_log_pad, dtb_pad, dn_norm.reshape(1, DN_DV))


def _gla_core(q, k, v, g, gate, nw, s_ref, o_ref, tb, hb, dk, dv):
    nsub = tb // SUB
    nc = tb // CHUNK
    per = CHUNK // SUB
    wk = hb * dk
    b = _dot(_chunk_cumsum_matrix(tb, True), g, HI)

    b3 = b.reshape(nsub, SUB, wk)
    last = b3[:, SUB - 1:SUB, :]
    prev = jnp.concatenate([jnp.zeros((1, 1, wk), jnp.float32), last[:nsub - 1]], axis=0)
    blk_id = _iota2((nsub, 1, wk), 0)
    base3 = jnp.where(blk_id % per == 0, 0.0, prev)
    base = jnp.broadcast_to(base3, (nsub, SUB, wk)).reshape(tb, wk)
    qt = q * jnp.exp(b - base)
    qe = q * jnp.exp(b)

    q3 = q.reshape(nsub, SUB, wk)
    k3 = k.reshape(nsub, SUB, wk)
    row = _iota2((tb, CHUNK), 0)
    col = _iota2((tb, CHUNK), 1)
    same_blk = ((row % CHUNK) // SUB) == (col // SUB)
    diag = [jnp.zeros((tb, CHUNK), jnp.float32) for _ in range(hb)]
    for j in range(SUB):
        bj = b3[:, j:j + 1, :]
        kj = k3[:, j:j + 1, :]
        e = jnp.exp(jnp.minimum(b3 - bj, 0.0))
        prod = (q3 * kj * e).reshape(tb, wk)
        sel = same_blk & ((col % SUB) == j) & ((row % SUB) >= j)
        for h in range(hb):
            colv = jnp.sum(prod[:, h * dk:(h + 1) * dk], axis=-1, keepdims=True)
            diag[h] = jnp.where(sel, colv, diag[h])

    rblk = _iota2((CHUNK, dk), 0) // SUB
    lhs_l, rhs_l, tr_l, v_l, qe_l, diag_l = [], [], [], [], [], []
    for c in range(nc):
        rs = slice(c * CHUNK, (c + 1) * CHUNK)
        for h in range(hb):
            ks_ = slice(h * dk, (h + 1) * dk)
            b_c = b[rs, ks_]
            k_c = k[rs, ks_]
            qt_c = qt[rs, ks_]
            lhs, rhs = [], []
            for i in range(1, per):
                n_i = b_c[i * SUB - 1:i * SUB]
                kt = k_c * jnp.exp(jnp.minimum(n_i - b_c, 0.0))
                rhs.append(jnp.where(rblk < i, kt, 0.0))
                lhs.append(jnp.where(rblk == i, qt_c, 0.0))
            lhs_l.append(jnp.concatenate(lhs, axis=1))
            rhs_l.append(jnp.concatenate(rhs, axis=1))
            b_last = b_c[CHUNK - 1:CHUNK]
            kg = k_c * jnp.exp(b_last - b_c)
            e_rows = jnp.broadcast_to(jnp.exp(b_last), (CHUNK, dk))
            tr_l.append(jnp.concatenate([kg, e_rows], axis=0).T)
            v_l.append(v[rs, h * dv:(h + 1) * dv])
            qe_l.append(qe[rs, ks_])
            diag_l.append(diag[h][rs])
    off = _bmm_nt(jnp.stack(lhs_l, axis=0), jnp.stack(rhs_l, axis=0))
    scores = jnp.stack(diag_l, axis=0) + off
    tr3 = jnp.stack(tr_l, axis=0)
    v3 = jnp.stack(v_l, axis=0)
    upd = _bmm(tr3[:, :, :CHUNK], v3)
    state = s_ref[...]
    states = []
    for c in range(nc):
        bs = slice(c * hb, (c + 1) * hb)
        states.append(state)
        state = state * tr3[bs, :, CHUNK:CHUNK + 1] + upd[bs]
    s_ref[...] = state
    o3 = _bmm(jnp.concatenate([jnp.stack(qe_l, axis=0), scores], axis=2),
              jnp.concatenate([jnp.concatenate(states, axis=0), v3], axis=1))
    for c in range(nc):
        rs = slice(c * CHUNK, (c + 1) * CHUNK)
        for h in range(hb):
            vs_ = slice(h * dv, (h + 1) * dv)
            o_ref[rs, vs_] = _head_out(o3[c * hb + h], nw, gate[rs, vs_]).astype(o_ref.dtype)


def _hg_kernel(q_ref, f_ref, i_ref, g_ref, lb_ref, nw_ref, o_ref, s_ref, *, tb):
    @pl.when(pl.program_id(1) == 0)
    def _():
        s_ref[...] = jnp.zeros_like(s_ref)

    lb = lb_ref[...]
    xf = f_ref[...]
    a = jnp.log(lb)
    cc = jnp.log1p(-lb) + _log_sigmoid(xf)
    log_f = jnp.maximum(a, cc) + jnp.log1p(jnp.exp(-jnp.abs(a - cc)))
    k_in = (1.0 - lb) * _sigmoid(-xf)
    q = q_ref[...] * (HG_DK ** -0.5)
    _gla_core(q, k_in, i_ref[...], log_f, g_ref[...], nw_ref[...], s_ref, o_ref, tb,
              HG_HB, HG_DK, HG_DV)


def hgrn2(proj_b, lb, hg_norm, tb):
    t = proj_b.shape[0]
    w = HG_HB * HG_DK

    def pspec(off):
        return pl.BlockSpec((tb, w), lambda g, i: (i, off // w + g))

    return pl.pallas_call(
        functools.partial(_hg_kernel, tb=tb),
        grid=(HG_HEADS // HG_HB, t // tb),
        in_specs=[pspec(B_HG_Q), pspec(B_HG_F), pspec(B_HG_I), pspec(B_HG_G),
                  pl.BlockSpec((1, w), lambda g, i: (0, g)),
                  pl.BlockSpec((1, HG_DV), lambda g, i: (0, 0))],
        out_specs=pl.BlockSpec((tb, w), lambda g, i: (i, g)),
        out_shape=jax.ShapeDtypeStruct((t, HG_V), BF16),
        scratch_shapes=[pltpu.VMEM((HG_HB, HG_DK, HG_DV), jnp.float32)],
        compiler_params=_cp(("parallel", "arbitrary")),
        name="hgrn2",
    )(proj_b, proj_b, proj_b, proj_b, lb.reshape(1, HG_K), hg_norm.reshape(1, HG_DV))


def _gl_kernel(q_ref, k_ref, v_ref, g_ref, sm_ref, w2_ref, bgk_ref, nw_ref, o_ref, s_ref, *, tb):
    @pl.when(pl.program_id(1) == 0)
    def _():
        s_ref[...] = jnp.zeros_like(s_ref)

    gk = _dot(sm_ref[...], w2_ref[...], HI) + bgk_ref[...]
    log_g = _log_sigmoid(gk) / GLA_GATE_NORM
    q = q_ref[...] * (GLA_DK ** -0.5)
    _gla_core(q, k_ref[...], v_ref[...], log_g, g_ref[...], nw_ref[...], s_ref, o_ref, tb,
              GLA_HB, GLA_DK, GLA_DV)


def gla(proj_b, small, w2_pad, b_gk, gla_norm, tb):
    t = proj_b.shape[0]
    wk = GLA_HB * GLA_DK
    wv = GLA_HB * GLA_DV
    return pl.pallas_call(
        functools.partial(_gl_kernel, tb=tb),
        grid=(GLA_HEADS // GLA_HB, t // tb),
        in_specs=[pl.BlockSpec((tb, wk), lambda g, i: (i, B_GL_Q // wk + g)),
                  pl.BlockSpec((tb, wk), lambda g, i: (i, B_GL_K // wk + g)),
                  pl.BlockSpec((tb, wv), lambda g, i: (i, B_GL_V // wv + g)),
                  pl.BlockSpec((tb, wv), lambda g, i: (i, B_GL_G // wv + g)),
                  pl.BlockSpec((tb, LANES), lambda g, i: (i, DN_GROUPS)),
                  pl.BlockSpec((LANES, wk), lambda g, i: (0, g)),
                  pl.BlockSpec((1, wk), lambda g, i: (0, g)),
                  pl.BlockSpec((1, GLA_DV), lambda g, i: (0, 0))],
        out_specs=pl.BlockSpec((tb, wv), lambda g, i: (i, g)),
        out_shape=jax.ShapeDtypeStruct((t, GLA_V), BF16),
        scratch_shapes=[pltpu.VMEM((GLA_HB, GLA_DK, GLA_DV), jnp.float32)],
        compiler_params=_cp(("parallel", "arbitrary")),
        name="gla",
    )(proj_b, proj_b, proj_b, proj_b, small, w2_pad, b_gk.reshape(1, GLA_K), gla_norm.reshape(1, GLA_DV))


def _prep_mixer_in(w_in):
    w_b = w_in[:, SRC_HG:SRC_GK].astype(BF16)
    w_c = w_in[:, SRC_GATES:].astype(BF16)
    d = w_in.shape[0]
    parts = []
    for g in range(DN_GROUPS):
        parts += [w_in[:, SRC_DN_B + g * DN_HB:SRC_DN_B + (g + 1) * DN_HB],
                  jnp.zeros((d, ALPHA_LANE - DN_HB), w_in.dtype),
                  w_in[:, SRC_DN_B + DN_HEADS + g * DN_HB:SRC_DN_B + DN_HEADS + (g + 1) * DN_HB],
                  jnp.zeros((d, LANES - ALPHA_LANE - DN_HB), w_in.dtype)]
    parts += [w_in[:, SRC_GK:SRC_GATES], jnp.zeros((d, LANES - GLA_RANK), w_in.dtype)]
    w_small = jnp.concatenate(parts, axis=1).astype(BF16)
    return w_b, w_c, w_small


def _pad_group_lanes(v):
    vg = v.reshape(DN_GROUPS, 1, DN_HB).astype(jnp.float32)
    return jnp.pad(vg, ((0, 0), (0, 0), (ALPHA_LANE, LANES - ALPHA_LANE - DN_HB)))


def _ffn(h, norm_w, w_in, w_out):
    u = rmsnorm(h, norm_w, BF16)
    a = swiglu_in(u, w_in, tm=2048, tn=256)
    return matmul_residual(a, w_out.astype(BF16), h, 0.5, tm=1024, tn=256)


def _mixer(h, norm_w, w_in, dn_conv, dn_a_log, dn_dt_bias, dn_norm, lb, hg_norm,
           gla_w_gk2, gla_b_gk, gla_norm, w_branch, w_out):
    w_b, w_c, w_small = _prep_mixer_in(w_in)
    u = rmsnorm(h, norm_w, BF16)
    proj_a = matmul(u, w_in, SRC_DN_B, jnp.float32, tm=1024, tn=512)
    proj_b = matmul(u, w_b, B_COLS, jnp.float32, tm=1024, tn=1024)
    gates = matmul(u, w_c, N_BRANCH * D_MODEL, jnp.float32, tm=1024, tn=1024)
    small = matmul(u, w_small, SMALL_COLS, jnp.float32, tm=1024, tn=SMALL_COLS)
    o_dn = deltanet(proj_a, small, dn_conv, _pad_group_lanes(dn_a_log), _pad_group_lanes(dn_dt_bias),
                    dn_norm, DN_TB)
    o_hg = hgrn2(proj_b, lb, hg_norm, GLA_TB)
    w2_pad = jnp.pad(gla_w_gk2.astype(jnp.float32), ((0, LANES - GLA_RANK), (0, 0)))
    o_gl = gla(proj_b, small, w2_pad, gla_b_gk, gla_norm, GLA_TB)
    y = branch_merge(o_dn, o_hg, o_gl, w_branch, gates, tm=1024, tn=512)
    return matmul_residual(y, w_out, h, 1.0, tm=1024, tn=512)


def kernel(x, norm_ffn1, ffn1_w_in, ffn1_w_out, norm_mix, w_in, dn_conv, dn_a_log, dn_dt_bias,
           dn_norm, hg_lower_bounds, hg_norm, gla_w_gk2, gla_b_gk, gla_norm, w_branch, w_out,
           norm_ffn2, ffn2_w_in, ffn2_w_out, norm_final):
    bsz, t, d = x.shape
    assert bsz == 1, "the chunked recurrences run over one sequence"
    lb_all = jnp.cumsum(jax.nn.softmax(hg_lower_bounds.astype(jnp.float32), axis=0), axis=0)
    lb_all = lb_all - lb_all[0]
    h = x.reshape(t, d)
    for l in range(DEPTH):
        h = _ffn(h, norm_ffn1[l], ffn1_w_in[l], ffn1_w_out[l])
        h = _mixer(h, norm_mix[l], w_in[l], dn_conv[l], dn_a_log[l], dn_dt_bias[l], dn_norm[l],
                   lb_all[l], hg_norm[l], gla_w_gk2[l], gla_b_gk[l], gla_norm[l],
                   w_branch[l], w_out[l])
        h = _ffn(h, norm_ffn2[l], ffn2_w_in[l], ffn2_w_out[l])
    return rmsnorm(h, norm_final, x.dtype).reshape(bsz, t, d)
```

```python
import functools

import jax
import jax.numpy as jnp
from jax import lax
from jax.experimental import pallas as pl
from jax.experimental.pallas import tpu as pltpu

D_MODEL = 4096
DEPTH = 2
CHUNK = 64
SUB = 16
EPS = 1e-6
DN_HEADS, DN_DK, DN_DV, DN_CONV = 16, 128, 128, 4
HG_HEADS, HG_DK, HG_DV = 8, 128, 128
GLA_HEADS, GLA_DK, GLA_DV, GLA_RANK = 4, 128, 256, 16
GLA_GATE_NORM = 16.0
D_FF = 11008
N_BRANCH = 3

DN_QK = DN_HEADS * DN_DK
DN_V = DN_HEADS * DN_DV
HG_K = HG_HEADS * HG_DK
HG_V = HG_HEADS * HG_DV
GLA_K = GLA_HEADS * GLA_DK
GLA_V = GLA_HEADS * GLA_DV

SRC_DN_B = 2 * DN_QK + 2 * DN_V
SRC_HG = SRC_DN_B + 2 * DN_HEADS
SRC_GK = SRC_HG + 2 * HG_K + 2 * HG_V + 2 * GLA_K + 2 * GLA_V
SRC_GATES = SRC_GK + GLA_RANK
A_DN_Q, A_DN_K, A_DN_V, A_DN_Z = 0, 2048, 4096, 6144
B_HG_Q, B_HG_F, B_HG_I, B_HG_G = 0, 1024, 2048, 3072
B_GL_Q, B_GL_K, B_GL_V, B_GL_G = 4096, 4608, 5120, 6144
B_COLS = SRC_GK - SRC_HG

DN_HB = 8
DN_GROUPS = DN_HEADS // DN_HB
HG_HB = 4
GLA_HB = 4
DN_TB = 128
GLA_TB = 256
LANES = 128
ALPHA_LANE = DN_HEADS
GK_LANE = 2 * DN_HEADS

VMEM_LIMIT = 56 * 1024 * 1024
HI = lax.Precision.HIGHEST
BF16 = jnp.bfloat16


def _cp(sem, vmem=VMEM_LIMIT):
    return pltpu.CompilerParams(dimension_semantics=sem, vmem_limit_bytes=vmem)


def _bf(x):
    return x if x.dtype == BF16 else x.astype(BF16)


def _dot(a, b, precision=None):
    return jnp.dot(a, b, preferred_element_type=jnp.float32, precision=precision)


def _bdot(a, b):
    return jnp.dot(_bf(a), _bf(b), preferred_element_type=jnp.float32)


def _bdot_nt(a, b):
    return lax.dot_general(_bf(a), _bf(b), (((1,), (1,)), ((), ())),
                           preferred_element_type=jnp.float32)


def _bmm(a, b):
    return jnp.einsum("bmk,bkn->bmn", _bf(a), _bf(b), preferred_element_type=jnp.float32)


def _bmm_nt(a, b):
    return jnp.einsum("bmk,bnk->bmn", _bf(a), _bf(b), preferred_element_type=jnp.float32)


def _sigmoid(x):
    return jax.nn.sigmoid(x)


def _softplus(x):
    return jnp.maximum(x, 0.0) + jnp.log1p(jnp.exp(-jnp.abs(x)))


def _log_sigmoid(x):
    return jnp.minimum(x, 0.0) - jnp.log1p(jnp.exp(-jnp.abs(x)))


def _resident(shape, index_map):
    return pl.BlockSpec(shape, index_map, pipeline_mode=pl.Buffered(1))


def _rmsnorm_kernel(x_ref, w_ref, o_ref):
    x = x_ref[...]
    y = x * lax.rsqrt(jnp.mean(x * x, axis=-1, keepdims=True) + EPS)
    o_ref[...] = (y * w_ref[...]).astype(o_ref.dtype)


def rmsnorm(x, w, out_dtype, tr=256):
    t, d = x.shape
    return pl.pallas_call(
        _rmsnorm_kernel,
        grid=(t // tr,),
        in_specs=[pl.BlockSpec((tr, d), lambda i: (i, 0)),
                  pl.BlockSpec((1, d), lambda i: (0, 0))],
        out_specs=pl.BlockSpec((tr, d), lambda i: (i, 0)),
        out_shape=jax.ShapeDtypeStruct((t, d), out_dtype),
        compiler_params=_cp(("parallel",)),
        name="rmsnorm",
    )(x, w.reshape(1, d))


def _mm_nt_kernel(a_ref, w_ref, o_ref):
    o_ref[...] = _bdot_nt(a_ref[...], w_ref[0]).astype(o_ref.dtype)


def _rows_spec(l, row0, rows, k, sublane=8):
    assert row0 % sublane == 0 and rows % sublane == 0
    return pl.BlockSpec((pl.Element(1), pl.Element(rows), pl.Element(k)),
                        lambda i, j: (l, pl.multiple_of(row0 + j * rows, sublane), 0))


def matmul_nt(a, w_t, l, row0, n, out_dtype, tm, tn):
    m, k = a.shape
    assert n % tn == 0
    return pl.pallas_call(
        _mm_nt_kernel,
        grid=(m // tm, n // tn),
        in_specs=[_resident((tm, k), lambda i, j: (i, 0)),
                  _rows_spec(l, row0, tn, k)],
        out_specs=pl.BlockSpec((tm, tn), lambda i, j: (i, j)),
        out_shape=jax.ShapeDtypeStruct((m, n), out_dtype),
        compiler_params=_cp(("parallel", "arbitrary")),
        name="matmul_nt",
    )(a, w_t)


def _small_kernel(a_ref, wba_ref, wgk_ref, o_ref):
    k = a_ref.shape[1]
    pad = jnp.zeros((LANES - GK_LANE - GLA_RANK, k), BF16)
    w = jnp.concatenate([_bf(wba_ref[0]), _bf(wgk_ref[0]), pad], axis=0)
    o_ref[...] = _bdot_nt(a_ref[...], w)


def small_proj(a, w_t, l, tm):
    m, k = a.shape
    return pl.pallas_call(
        _small_kernel,
        grid=(m // tm, 1),
        in_specs=[pl.BlockSpec((tm, k), lambda i, j: (i, 0)),
                  _rows_spec(l, SRC_DN_B, 2 * DN_HEADS, k),
                  _rows_spec(l, SRC_GK, GLA_RANK, k)],
        out_specs=pl.BlockSpec((tm, LANES), lambda i, j: (i, 0)),
        out_shape=jax.ShapeDtypeStruct((m, LANES), jnp.float32),
        compiler_params=_cp(("parallel", "arbitrary")),
        name="small_proj",
    )(a, w_t, w_t)


def _swiglu_kernel(a_ref, wg_ref, wu_ref, o_ref, *, tn):
    w = jnp.concatenate([_bf(wg_ref[...]), _bf(wu_ref[...])], axis=1)
    r = _bdot(a_ref[...], w)
    g = r[:, :tn]
    u = r[:, tn:]
    o_ref[...] = (g * _sigmoid(g) * u).astype(o_ref.dtype)


def swiglu_in(a, w_in, l, tm, tn):
    m, k = a.shape
    nf = w_in.shape[2] // 2
    nj = nf // tn
    return pl.pallas_call(
        functools.partial(_swiglu_kernel, tn=tn),
        grid=(m // tm, nj),
        in_specs=[_resident((tm, k), lambda i, j: (i, 0)),
                  pl.BlockSpec((None, k, tn), lambda i, j: (l, 0, j)),
                  pl.BlockSpec((None, k, tn), lambda i, j: (l, 0, nj + j))],
        out_specs=pl.BlockSpec((tm, tn), lambda i, j: (i, j)),
        out_shape=jax.ShapeDtypeStruct((m, nf), BF16),
        compiler_params=_cp(("parallel", "arbitrary")),
        name="swiglu_in",
    )(a, w_in, w_in)


def _mm_res_kernel(a_ref, w_ref, r_ref, o_ref, *, scale, ksplit):
    k = a_ref.shape[1]
    kc = k // ksplit
    acc = _bdot(a_ref[:, :kc], w_ref[:kc, :])
    for s in range(1, ksplit):
        acc = acc + _bdot(a_ref[:, s * kc:(s + 1) * kc], w_ref[s * kc:(s + 1) * kc, :])
    o_ref[...] = r_ref[...] + scale * acc


def matmul_residual(a, w, l, res, scale, tm, tn, ksplit=1, vmem=VMEM_LIMIT):
    m, k = a.shape
    n = w.shape[2]
    return pl.pallas_call(
        functools.partial(_mm_res_kernel, scale=scale, ksplit=ksplit),
        grid=(m // tm, n // tn),
        in_specs=[_resident((tm, k), lambda i, j: (i, 0)),
                  pl.BlockSpec((None, k, tn), lambda i, j: (l, 0, j)),
                  pl.BlockSpec((tm, tn), lambda i, j: (i, j))],
        out_specs=pl.BlockSpec((tm, tn), lambda i, j: (i, j)),
        out_shape=jax.ShapeDtypeStruct((m, n), jnp.float32),
        compiler_params=_cp(("parallel", "arbitrary"), vmem),
        name="matmul_residual",
    )(a, w, res)


def _merge_kernel(a0_ref, a1_ref, a2_ref, w0_ref, w1_ref, w2_ref,
                  g0_ref, g1_ref, g2_ref, o_ref):
    f32 = jnp.float32
    y = _sigmoid(g0_ref[...].astype(f32)) * _bdot(a0_ref[...], w0_ref[...])
    y = y + _sigmoid(g1_ref[...].astype(f32)) * _bdot(a1_ref[...], w1_ref[...])
    y = y + _sigmoid(g2_ref[...].astype(f32)) * _bdot(a2_ref[...], w2_ref[...])
    o_ref[...] = y.astype(o_ref.dtype)


def branch_merge(o_dn, o_hg, o_gl, w_branch, l, gates, tm, tn):
    m = o_dn.shape[0]
    n = D_MODEL
    per_branch = D_MODEL // tn
    hg_blk = DN_V // HG_V
    gl_blk = (DN_V + HG_V) // GLA_V

    def gate_spec(b):
        return pl.BlockSpec((tm, tn), lambda i, j: (i, b * per_branch + j))

    return pl.pallas_call(
        _merge_kernel,
        grid=(m // tm, n // tn),
        in_specs=[_resident((tm, DN_V), lambda i, j: (i, 0)),
                  _resident((tm, HG_V), lambda i, j: (i, 0)),
                  _resident((tm, GLA_V), lambda i, j: (i, 0)),
                  pl.BlockSpec((None, DN_V, tn), lambda i, j: (l, 0, j)),
                  pl.BlockSpec((None, HG_V, tn), lambda i, j: (l, hg_blk, j)),
                  pl.BlockSpec((None, GLA_V, tn), lambda i, j: (l, gl_blk, j)),
                  gate_spec(0), gate_spec(1), gate_spec(2)],
        out_specs=pl.BlockSpec((tm, tn), lambda i, j: (i, j)),
        out_shape=jax.ShapeDtypeStruct((m, n), BF16),
        compiler_params=_cp(("parallel", "arbitrary")),
        name="branch_merge",
    )(o_dn, o_hg, o_gl, w_branch, w_branch, w_branch, gates, gates, gates)


def _iota2(shape, dim):
    return lax.broadcasted_iota(jnp.int32, shape, dim)


def _chunk_cumsum_matrix(tb, lower):
    r = _iota2((tb, tb), 0)
    c = _iota2((tb, tb), 1)
    same = (r // CHUNK) == (c // CHUNK)
    tri = (c <= r) if lower else (r <= c)
    return jnp.where(same & tri, 1.0, 0.0).astype(jnp.float32)


def _head_out(o, nw, gate):
    y = o * lax.rsqrt(jnp.mean(o * o, axis=-1, keepdims=True) + EPS) * nw
    return y * (gate * _sigmoid(gate))


def _chunks(x, nc):
    return [x[c * CHUNK:(c + 1) * CHUNK] for c in range(nc)]


def _unit_lower_inverse(a_strict):
    r = _iota2((CHUNK, CHUNK), 0)
    c = _iota2((CHUNK, CHUNK), 1)
    eye = jnp.where(r == c, 1.0, 0.0).astype(jnp.float32)
    blk = (r // SUB) == (c // SUB)
    d = jnp.where(blk, a_strict, 0.0)
    e = a_strict - d
    d2 = _bmm(d, d)
    d4 = _bmm(d2, d2)
    d8 = _bmm(d4, d4)
    p = eye - d
    p = p + _bmm(p, d2)
    p = p + _bmm(p, d4)
    p = p + _bmm(p, d8)
    n = _bmm(p, e)
    n2 = _bmm(n, n)
    q = eye - n + n2 - _bmm(n, n2)
    return q, p


def _dn_kernel(q_ref, k_ref, v_ref, z_ref, sm_ref, cq_ref, ck_ref, cv_ref,
               alog_ref, dtb_ref, nw_ref, o_ref, s_ref, carry_ref, *, tb):
    g = pl.program_id(0)
    t = pl.program_id(1)

    @pl.when(t == 0)
    def _():
        s_ref[...] = jnp.zeros_like(s_ref)
        carry_ref[...] = jnp.zeros_like(carry_ref)

    nc = tb // CHUNK
    hb = DN_HB

    def conv_silu(x_ref, cw_ref, idx):
        x = x_ref[...]
        ext = jnp.concatenate([carry_ref[idx], x], axis=0)
        w = cw_ref[...]
        y = x * w[DN_CONV - 1:DN_CONV]
        for s in range(1, DN_CONV):
            y = y + pltpu.roll(ext, s, 0)[8:] * w[DN_CONV - 1 - s:DN_CONV - s]
        carry_ref[idx] = x[tb - 8:]
        return y * _sigmoid(y)

    yq = conv_silu(q_ref, cq_ref, 0)
    yk = conv_silu(k_ref, ck_ref, 1)
    yv = conv_silu(v_ref, cv_ref, 2)

    sm = sm_ref[...]
    sm = jnp.where(g == 0, sm, pltpu.roll(sm, LANES - hb, 1))
    lane = _iota2((1, LANES), 1)
    is_alpha = (lane >= ALPHA_LANE) & (lane < ALPHA_LANE + hb)
    a_coef = jnp.where(is_alpha, jnp.exp(alog_ref[0]), 0.0)
    beta_slab = _sigmoid(sm)
    la_slab = -a_coef * _softplus(sm + dtb_ref[0])
    bcum_slab = _dot(_chunk_cumsum_matrix(tb, True), la_slab, HI)
    la_t = la_slab.T[ALPHA_LANE:ALPHA_LANE + 8]
    bcum_t = _dot(la_t, _chunk_cumsum_matrix(tb, False), HI)

    def stack_heads(per_head):
        return jnp.stack([per_head[h][c] for c in range(nc) for h in range(hb)], axis=0)

    qs, ks, vs, betas, bcs = [], [], [], [], []
    for h in range(hb):
        hs = slice(h * DN_DK, (h + 1) * DN_DK)
        qh, kh = yq[:, hs], yk[:, hs]
        qn = qh * lax.rsqrt(jnp.sum(qh * qh, axis=-1, keepdims=True) + EPS) * (DN_DK ** -0.5)
        kn = kh * lax.rsqrt(jnp.sum(kh * kh, axis=-1, keepdims=True) + EPS)
        qs.append(_chunks(qn, nc))
        ks.append(_chunks(kn, nc))
        vs.append(_chunks(yv[:, hs], nc))
        betas.append(_chunks(beta_slab[:, h:h + 1], nc))
        bcs.append(_chunks(bcum_slab[:, ALPHA_LANE + h:ALPHA_LANE + h + 1], nc))
    q3, k3, v3 = stack_heads(qs), stack_heads(ks), stack_heads(vs)
    beta3, bc3 = stack_heads(betas), stack_heads(bcs)
    br3 = jnp.stack([bcum_t[h:h + 1, c * CHUNK:(c + 1) * CHUNK]
                     for c in range(nc) for h in range(hb)], axis=0)

    r64 = _iota2((CHUNK, CHUNK), 0)
    c64 = _iota2((CHUNK, CHUNK), 1)
    causal = c64 <= r64
    strict = c64 < r64
    ebc3 = jnp.exp(bc3)
    kb3 = k3 * beta3
    decay3 = jnp.where(causal, jnp.exp(jnp.where(causal, bc3 - br3, 0.0)), 0.0)
    kq3 = _bmm_nt(jnp.concatenate([kb3, q3], axis=1), k3)
    kk3 = kq3[:, :CHUNK] * decay3
    qk3 = kq3[:, CHUNK:] * decay3
    qmat, pmat = _unit_lower_inverse(jnp.where(strict, kk3, 0.0))
    rhs3 = jnp.concatenate([v3 * beta3, kb3 * ebc3], axis=2)
    sol3 = _bmm(qmat, _bmm(pmat, rhs3))
    u3 = sol3[:, :, :DN_DV]
    wq3 = jnp.concatenate([sol3[:, :, DN_DV:], q3 * ebc3], axis=1)
    b_last3 = bc3[:, CHUNK - 1:CHUNK, :]
    kd3 = k3 * jnp.exp(b_last3 - bc3)
    kdt3 = jnp.stack([kd3[b].T for b in range(nc * hb)], axis=0)
    a_last3 = jnp.exp(b_last3)

    nw = nw_ref[...]
    z_all = z_ref[...]
    state = s_ref[...]
    for c in range(nc):
        bs = slice(c * hb, (c + 1) * hb)
        rs = slice(c * CHUNK, (c + 1) * CHUNK)
        ws = _bmm(wq3[bs], state)
        v_new = u3[bs] - ws[:, :CHUNK]
        o = ws[:, CHUNK:] + _bmm(qk3[bs], v_new)
        state = state * a_last3[bs] + _bmm(kdt3[bs], v_new)
        for h in range(hb):
            hs = slice(h * DN_DK, (h + 1) * DN_DK)
            o_ref[rs, hs] = _head_out(o[h], nw, z_all[rs, hs]).astype(o_ref.dtype)
    s_ref[...] = state


def deltanet(proj_a, small, dn_conv, l, a_log_pad, dtb_pad, dn_norm, tb):
    t = proj_a.shape[0]
    w = DN_HB * DN_DK
    qb, kb_, vb, zb = (A_DN_Q // w, A_DN_K // w, A_DN_V // w, A_DN_Z // w)

    def pspec(base):
        return pl.BlockSpec((tb, w), lambda g, i: (i, base + g))

    def cspec(base):
        return pl.BlockSpec((None, DN_CONV, w), lambda g, i: (l, 0, base + g))

    pad_spec = pl.BlockSpec((1, 1, LANES), lambda g, i: (g, 0, 0))
    assert DN_GROUPS == 2, "the kernel selects its group's lanes with one rotate"
    return pl.pallas_call(
        functools.partial(_dn_kernel, tb=tb),
        grid=(DN_GROUPS, t // tb),
        in_specs=[pspec(qb), pspec(kb_), pspec(vb), pspec(zb),
                  pl.BlockSpec((tb, LANES), lambda g, i: (i, 0)),
                  cspec(qb), cspec(kb_), cspec(vb),
                  pad_spec, pad_spec,
                  pl.BlockSpec((1, DN_DV), lambda g, i: (0, 0))],
        out_specs=pl.BlockSpec((tb, w), lambda g, i: (i, g)),
        out_shape=jax.ShapeDtypeStruct((t, DN_V), BF16),
        scratch_shapes=[pltpu.VMEM((DN_HB, DN_DK, DN_DV), jnp.float32),
                        pltpu.VMEM((3, 8, w), jnp.float32)],
        compiler_params=_cp(("parallel", "arbitrary")),
        name="deltanet",
    )(proj_a, proj_a, proj_a, proj_a, small, dn_conv, dn_conv, dn_conv,
      a_log_pad, dtb_pad, dn_norm.reshape(1, DN_DV))


def _gla_core(q, k, v, g, gate, nw, s_ref, o_ref, tb, hb, dk, dv):
    nsub = tb // SUB
    nc = tb // CHUNK
    per = CHUNK // SUB
    wk = hb * dk
    b = _dot(_chunk_cumsum_matrix(tb, True), g, HI)

    b3 = b.reshape(nsub, SUB, wk)
    last = b3[:, SUB - 1:SUB, :]
    prev = jnp.concatenate([jnp.zeros((1, 1, wk), jnp.float32), last[:nsub - 1]], axis=0)
    blk_id = _iota2((nsub, 1, wk), 0)
    base3 = jnp.where(blk_id % per == 0, 0.0, prev)
    base = jnp.broadcast_to(base3, (nsub, SUB, wk)).reshape(tb, wk)
    qt = q * jnp.exp(b - base)
    qe = q * jnp.exp(b)

    q3 = q.reshape(nsub, SUB, wk)
    k3 = k.reshape(nsub, SUB, wk)
    row = _iota2((tb, CHUNK), 0)
    col = _iota2((tb, CHUNK), 1)
    same_blk = ((row % CHUNK) // SUB) == (col // SUB)
    diag = [jnp.zeros((tb, CHUNK), jnp.float32) for _ in range(hb)]
    for j in range(SUB):
        bj = b3[:, j:j + 1, :]
        kj = k3[:, j:j + 1, :]
        e = jnp.exp(jnp.minimum(b3 - bj, 0.0))
        prod = (q3 * kj * e).reshape(tb, wk)
        sel = same_blk & ((col % SUB) == j) & ((row % SUB) >= j)
        for h in range(hb):
            colv = jnp.sum(prod[:, h * dk:(h + 1) * dk], axis=-1, keepdims=True)
            diag[h] = jnp.where(sel, colv, diag[h])

    rblk = _iota2((CHUNK, dk), 0) // SUB
    lhs_l, rhs_l, tr_l, v_l, qe_l, diag_l = [], [], [], [], [], []
    for c in range(nc):
        rs = slice(c * CHUNK, (c + 1) * CHUNK)
        for h in range(hb):
            ks_ = slice(h * dk, (h + 1) * dk)
            b_c = b[rs, ks_]
            k_c = k[rs, ks_]
            qt_c = qt[rs, ks_]
            lhs, rhs = [], []
            for i in range(1, per):
                n_i = b_c[i * SUB - 1:i * SUB]
                kt = k_c * jnp.exp(jnp.minimum(n_i - b_c, 0.0))
                rhs.append(jnp.where(rblk < i, kt, 0.0))
                lhs.append(jnp.where(rblk == i, qt_c, 0.0))
            lhs_l.append(jnp.concatenate(lhs, axis=1))
            rhs_l.append(jnp.concatenate(rhs, axis=1))
            b_last = b_c[CHUNK - 1:CHUNK]
            kg = k_c * jnp.exp(b_last - b_c)
            e_rows = jnp.broadcast_to(jnp.exp(b_last), (CHUNK, dk))
            tr_l.append(jnp.concatenate([kg, e_rows], axis=0).T)
            v_l.append(v[rs, h * dv:(h + 1) * dv])
            qe_l.append(qe[rs, ks_])
            diag_l.append(diag[h][rs])
    off = _bmm_nt(jnp.stack(lhs_l, axis=0), jnp.stack(rhs_l, axis=0))
    scores = jnp.stack(diag_l, axis=0) + off
    tr3 = jnp.stack(tr_l, axis=0)
    v3 = jnp.stack(v_l, axis=0)
    upd = _bmm(tr3[:, :, :CHUNK], v3)
    state = s_ref[...]
    states = []
    for c in range(nc):
        bs = slice(c * hb, (c + 1) * hb)
        states.append(state)
        state = state * tr3[bs, :, CHUNK:CHUNK + 1] + upd[bs]
    s_ref[...] = state
    o3 = _bmm(jnp.concatenate([jnp.stack(qe_l, axis=0), scores], axis=2),
              jnp.concatenate([jnp.concatenate(states, axis=0), v3], axis=1))
    for c in range(nc):
        rs = slice(c * CHUNK, (c + 1) * CHUNK)
        for h in range(hb):
            vs_ = slice(h * dv, (h + 1) * dv)
            o_ref[rs, vs_] = _head_out(o3[c * hb + h], nw, gate[rs, vs_]).astype(o_ref.dtype)


def _hg_kernel(q_ref, f_ref, i_ref, g_ref, lb_ref, nw_ref, o_ref, s_ref, *, tb):
    @pl.when(pl.program_id(1) == 0)
    def _():
        s_ref[...] = jnp.zeros_like(s_ref)

    lb = lb_ref[...]
    xf = f_ref[...]
    a = jnp.log(lb)
    cc = jnp.log1p(-lb) + _log_sigmoid(xf)
    log_f = jnp.maximum(a, cc) + jnp.log1p(jnp.exp(-jnp.abs(a - cc)))
    k_in = (1.0 - lb) * _sigmoid(-xf)
    q = q_ref[...] * (HG_DK ** -0.5)
    _gla_core(q, k_in, i_ref[...], log_f, g_ref[...], nw_ref[...], s_ref, o_ref, tb,
              HG_HB, HG_DK, HG_DV)


def hgrn2(proj_b, lb, hg_norm, tb):
    t = proj_b.shape[0]
    w = HG_HB * HG_DK

    def pspec(off):
        return pl.BlockSpec((tb, w), lambda g, i: (i, off // w + g))

    return pl.pallas_call(
        functools.partial(_hg_kernel, tb=tb),
        grid=(HG_HEADS // HG_HB, t // tb),
        in_specs=[pspec(B_HG_Q), pspec(B_HG_F), pspec(B_HG_I), pspec(B_HG_G),
                  pl.BlockSpec((1, w), lambda g, i: (0, g)),
                  pl.BlockSpec((1, HG_DV), lambda g, i: (0, 0))],
        out_specs=pl.BlockSpec((tb, w), lambda g, i: (i, g)),
        out_shape=jax.ShapeDtypeStruct((t, HG_V), BF16),
        scratch_shapes=[pltpu.VMEM((HG_HB, HG_DK, HG_DV), jnp.float32)],
        compiler_params=_cp(("parallel", "arbitrary")),
        name="hgrn2",
    )(proj_b, proj_b, proj_b, proj_b, lb.reshape(1, HG_K), hg_norm.reshape(1, HG_DV))


def _gl_kernel(q_ref, k_ref, v_ref, g_ref, sm_ref, w2_ref, bgk_ref, nw_ref, o_ref, s_ref, *, tb):
    @pl.when(pl.program_id(1) == 0)
    def _():
        s_ref[...] = jnp.zeros_like(s_ref)

    gk = _dot(sm_ref[...], w2_ref[...], HI) + bgk_ref[...]
    log_g = _log_sigmoid(gk) / GLA_GATE_NORM
    q = q_ref[...] * (GLA_DK ** -0.5)
    _gla_core(q, k_ref[...], v_ref[...], log_g, g_ref[...], nw_ref[...], s_ref, o_ref, tb,
              GLA_HB, GLA_DK, GLA_DV)


def gla(proj_b, small, w2_pad, b_gk, gla_norm, tb):
    t = proj_b.shape[0]
    wk = GLA_HB * GLA_DK
    wv = GLA_HB * GLA_DV
    return pl.pallas_call(
        functools.partial(_gl_kernel, tb=tb),
        grid=(GLA_HEADS // GLA_HB, t // tb),
        in_specs=[pl.BlockSpec((tb, wk), lambda g, i: (i, B_GL_Q // wk + g)),
                  pl.BlockSpec((tb, wk), lambda g, i: (i, B_GL_K // wk + g)),
                  pl.BlockSpec((tb, wv), lambda g, i: (i, B_GL_V // wv + g)),
                  pl.BlockSpec((tb, wv), lambda g, i: (i, B_GL_G // wv + g)),
                  pl.BlockSpec((tb, LANES), lambda g, i: (i, 0)),
                  pl.BlockSpec((LANES, wk), lambda g, i: (0, g)),
                  pl.BlockSpec((1, wk), lambda g, i: (0, g)),
                  pl.BlockSpec((1, GLA_DV), lambda g, i: (0, 0))],
        out_specs=pl.BlockSpec((tb, wv), lambda g, i: (i, g)),
        out_shape=jax.ShapeDtypeStruct((t, GLA_V), BF16),
        scratch_shapes=[pltpu.VMEM((GLA_HB, GLA_DK, GLA_DV), jnp.float32)],
        compiler_params=_cp(("parallel", "arbitrary")),
        name="gla",
    )(proj_b, proj_b, proj_b, proj_b, small, w2_pad, b_gk.reshape(1, GLA_K), gla_norm.reshape(1, GLA_DV))


def _pad_group_lanes(v):
    vg = v.reshape(DN_GROUPS, 1, DN_HB).astype(jnp.float32)
    return jnp.pad(vg, ((0, 0), (0, 0), (ALPHA_LANE, LANES - ALPHA_LANE - DN_HB)))


FFN_OUT_VMEM = 60 * 1024 * 1024


def _ffn(h, norm_w, w_in, w_out, l):
    u = rmsnorm(h, norm_w, BF16)
    a = swiglu_in(u, w_in, l, tm=2048, tn=256)
    return matmul_residual(a, w_out, l, h, 0.5, tm=1024, tn=256, ksplit=2, vmem=FFN_OUT_VMEM)


def _mixer(h, l, norm_w, w_t, dn_conv, dn_a_log, dn_dt_bias, dn_norm, lb, hg_norm,
           gla_w_gk2, gla_b_gk, gla_norm, w_branch, w_out):
    u = rmsnorm(h, norm_w, BF16)
    proj_a = matmul_nt(u, w_t, l, 0, SRC_DN_B, jnp.float32, tm=2048, tn=512)
    proj_b = matmul_nt(u, w_t, l, SRC_HG, B_COLS, jnp.float32, tm=2048, tn=512)
    gates = matmul_nt(u, w_t, l, SRC_GATES, N_BRANCH * D_MODEL, BF16, tm=2048, tn=512)
    small = small_proj(u, w_t, l, tm=1024)
    o_dn = deltanet(proj_a, small, dn_conv, l, _pad_group_lanes(dn_a_log), _pad_group_lanes(dn_dt_bias),
                    dn_norm, DN_TB)
    o_hg = hgrn2(proj_b, lb, hg_norm, GLA_TB)
    w2_pad = jnp.pad(gla_w_gk2.astype(jnp.float32), ((GK_LANE, LANES - GK_LANE - GLA_RANK), (0, 0)))
    o_gl = gla(proj_b, small, w2_pad, gla_b_gk, gla_norm, GLA_TB)
    y = branch_merge(o_dn, o_hg, o_gl, w_branch, l, gates, tm=2048, tn=256)
    return matmul_residual(y, w_out, l, h, 1.0, tm=2048, tn=256)


def kernel(x, norm_ffn1, ffn1_w_in, ffn1_w_out, norm_mix, w_in, dn_conv, dn_a_log, dn_dt_bias,
           dn_norm, hg_lower_bounds, hg_norm, gla_w_gk2, gla_b_gk, gla_norm, w_branch, w_out,
           norm_ffn2, ffn2_w_in, ffn2_w_out, norm_final):
    bsz, t, d = x.shape
    assert bsz == 1, "the chunked recurrences run over one sequence"
    lb_all = jnp.cumsum(jax.nn.softmax(hg_lower_bounds.astype(jnp.float32), axis=0), axis=0)
    lb_all = lb_all - lb_all[0]
    w_t = jnp.swapaxes(w_in, 1, 2)
    h = x.reshape(t, d)
    for l in range(DEPTH):
        h = _ffn(h, norm_ffn1[l], ffn1_w_in, ffn1_w_out, l)
        h = _mixer(h, l, norm_mix[l], w_t, dn_conv, dn_a_log[l], dn_dt_bias[l],
                   dn_norm[l], lb_all[l], hg_norm[l], gla_w_gk2[l], gla_b_gk[l], gla_norm[l],
                   w_branch, w_out)
        h = _ffn(h, norm_ffn2[l], ffn2_w_in, ffn2_w_out, l)
    return rmsnorm(h, norm_final, x.dtype).reshape(bsz, t, d)
```

```python
import functools

import jax
import jax.numpy as jnp
from jax import lax
from jax.experimental import pallas as pl
from jax.experimental.pallas import tpu as pltpu

D_MODEL = 4096
DEPTH = 2
CHUNK = 64
SUB = 16
GLA_SUB = 8
EPS = 1e-6
DN_HEADS, DN_DK, DN_DV, DN_CONV = 16, 128, 128, 4
HG_HEADS, HG_DK, HG_DV = 8, 128, 128
GLA_HEADS, GLA_DK, GLA_DV, GLA_RANK = 4, 128, 256, 16
GLA_GATE_NORM = 16.0
D_FF = 11008
N_BRANCH = 3

DN_QK = DN_HEADS * DN_DK
DN_V = DN_HEADS * DN_DV
HG_K = HG_HEADS * HG_DK
HG_V = HG_HEADS * HG_DV
GLA_K = GLA_HEADS * GLA_DK
GLA_V = GLA_HEADS * GLA_DV

SRC_DN_B = 2 * DN_QK + 2 * DN_V
SRC_HG = SRC_DN_B + 2 * DN_HEADS
SRC_GK = SRC_HG + 2 * HG_K + 2 * HG_V + 2 * GLA_K + 2 * GLA_V
SRC_GATES = SRC_GK + GLA_RANK
A_DN_Q, A_DN_K, A_DN_V, A_DN_Z = 0, 2048, 4096, 6144
B_HG_Q, B_HG_F, B_HG_I, B_HG_G = 0, 1024, 2048, 3072
B_GL_Q, B_GL_K, B_GL_V, B_GL_G = 4096, 4608, 5120, 6144
B_COLS = SRC_GK - SRC_HG

DN_HB = 8
DN_GROUPS = DN_HEADS // DN_HB
HG_HB = 4
GLA_HB = 4
DN_TB = 256
GLA_TB = 256
LANES = 128
ALPHA_LANE = DN_HEADS
GK_LANE = 2 * DN_HEADS

VMEM_LIMIT = 56 * 1024 * 1024
HI = lax.Precision.HIGHEST
BF16 = jnp.bfloat16


def _cp(sem, vmem=VMEM_LIMIT):
    return pltpu.CompilerParams(dimension_semantics=sem, vmem_limit_bytes=vmem)


def _bf(x):
    return x if x.dtype == BF16 else x.astype(BF16)


def _dot(a, b, precision=None):
    return jnp.dot(a, b, preferred_element_type=jnp.float32, precision=precision)


def _bdot(a, b):
    return jnp.dot(_bf(a), _bf(b), preferred_element_type=jnp.float32)


def _bdot_nt(a, b):
    return lax.dot_general(_bf(a), _bf(b), (((1,), (1,)), ((), ())),
                           preferred_element_type=jnp.float32)


def _bmm(a, b):
    return jnp.einsum("bmk,bkn->bmn", _bf(a), _bf(b), preferred_element_type=jnp.float32)


def _bmm_nt(a, b):
    return jnp.einsum("bmk,bnk->bmn", _bf(a), _bf(b), preferred_element_type=jnp.float32)


def _sigmoid(x):
    return jax.nn.sigmoid(x)


def _softplus(x):
    return jnp.maximum(x, 0.0) + jnp.log1p(jnp.exp(-jnp.abs(x)))


def _log_sigmoid(x):
    return jnp.minimum(x, 0.0) - jnp.log1p(jnp.exp(-jnp.abs(x)))


def _resident(shape, index_map):
    return pl.BlockSpec(shape, index_map, pipeline_mode=pl.Buffered(1))


def _rmsnorm_kernel(x_ref, w_ref, o_ref):
    x = x_ref[...]
    y = x * lax.rsqrt(jnp.mean(x * x, axis=-1, keepdims=True) + EPS)
    o_ref[...] = (y * w_ref[...]).astype(o_ref.dtype)


def rmsnorm(x, w, out_dtype, tr=256):
    t, d = x.shape
    return pl.pallas_call(
        _rmsnorm_kernel,
        grid=(t // tr,),
        in_specs=[pl.BlockSpec((tr, d), lambda i: (i, 0)),
                  pl.BlockSpec((1, d), lambda i: (0, 0))],
        out_specs=pl.BlockSpec((tr, d), lambda i: (i, 0)),
        out_shape=jax.ShapeDtypeStruct((t, d), out_dtype),
        compiler_params=_cp(("parallel",)),
        name="rmsnorm",
    )(x, w.reshape(1, d))


def _mm_nt_kernel(a_ref, w_ref, o_ref):
    o_ref[...] = _bdot_nt(a_ref[...], w_ref[0]).astype(o_ref.dtype)


def _rows_spec(l, row0, rows, k, sublane=8):
    assert row0 % sublane == 0 and rows % sublane == 0
    return pl.BlockSpec((pl.Element(1), pl.Element(rows), pl.Element(k)),
                        lambda i, j: (l, pl.multiple_of(row0 + j * rows, sublane), 0))


def matmul_nt(a, w_t, l, row0, n, out_dtype, tm, tn):
    m, k = a.shape
    assert n % tn == 0
    return pl.pallas_call(
        _mm_nt_kernel,
        grid=(m // tm, n // tn),
        in_specs=[_resident((tm, k), lambda i, j: (i, 0)),
                  _rows_spec(l, row0, tn, k)],
        out_specs=pl.BlockSpec((tm, tn), lambda i, j: (i, j)),
        out_shape=jax.ShapeDtypeStruct((m, n), out_dtype),
        compiler_params=_cp(("parallel", "arbitrary")),
        name="matmul_nt",
    )(a, w_t)


def _small_kernel(a_ref, wba_ref, wgk_ref, o_ref):
    k = a_ref.shape[1]
    pad = jnp.zeros((LANES - GK_LANE - GLA_RANK, k), BF16)
    w = jnp.concatenate([_bf(wba_ref[0]), _bf(wgk_ref[0]), pad], axis=0)
    o_ref[...] = _bdot_nt(a_ref[...], w)


def small_proj(a, w_t, l, tm):
    m, k = a.shape
    return pl.pallas_call(
        _small_kernel,
        grid=(m // tm, 1),
        in_specs=[pl.BlockSpec((tm, k), lambda i, j: (i, 0)),
                  _rows_spec(l, SRC_DN_B, 2 * DN_HEADS, k),
                  _rows_spec(l, SRC_GK, GLA_RANK, k)],
        out_specs=pl.BlockSpec((tm, LANES), lambda i, j: (i, 0)),
        out_shape=jax.ShapeDtypeStruct((m, LANES), jnp.float32),
        compiler_params=_cp(("parallel", "arbitrary")),
        name="small_proj",
    )(a, w_t, w_t)


def _swiglu_kernel(a_ref, wg_ref, wu_ref, o_ref, *, tn):
    w = jnp.concatenate([_bf(wg_ref[...]), _bf(wu_ref[...])], axis=1)
    r = _bdot(a_ref[...], w)
    g = r[:, :tn]
    u = r[:, tn:]
    o_ref[...] = (g * _sigmoid(g) * u).astype(o_ref.dtype)


def swiglu_in(a, w_in, l, tm, tn):
    m, k = a.shape
    nf = w_in.shape[2] // 2
    nj = nf // tn
    return pl.pallas_call(
        functools.partial(_swiglu_kernel, tn=tn),
        grid=(m // tm, nj),
        in_specs=[_resident((tm, k), lambda i, j: (i, 0)),
                  pl.BlockSpec((None, k, tn), lambda i, j: (l, 0, j)),
                  pl.BlockSpec((None, k, tn), lambda i, j: (l, 0, nj + j))],
        out_specs=pl.BlockSpec((tm, tn), lambda i, j: (i, j)),
        out_shape=jax.ShapeDtypeStruct((m, nf), BF16),
        compiler_params=_cp(("parallel", "arbitrary")),
        name="swiglu_in",
    )(a, w_in, w_in)


def _mm_res_kernel(a_ref, w_ref, r_ref, o_ref, *, scale, ksplit):
    k = a_ref.shape[1]
    kc = k // ksplit
    acc = _bdot(a_ref[:, :kc], w_ref[:kc, :])
    for s in range(1, ksplit):
        acc = acc + _bdot(a_ref[:, s * kc:(s + 1) * kc], w_ref[s * kc:(s + 1) * kc, :])
    o_ref[...] = r_ref[...] + scale * acc


def matmul_residual(a, w, l, res, scale, tm, tn, ksplit=1, vmem=VMEM_LIMIT):
    m, k = a.shape
    n = w.shape[2]
    return pl.pallas_call(
        functools.partial(_mm_res_kernel, scale=scale, ksplit=ksplit),
        grid=(m // tm, n // tn),
        in_specs=[_resident((tm, k), lambda i, j: (i, 0)),
                  pl.BlockSpec((None, k, tn), lambda i, j: (l, 0, j)),
                  pl.BlockSpec((tm, tn), lambda i, j: (i, j))],
        out_specs=pl.BlockSpec((tm, tn), lambda i, j: (i, j)),
        out_shape=jax.ShapeDtypeStruct((m, n), jnp.float32),
        compiler_params=_cp(("parallel", "arbitrary"), vmem),
        name="matmul_residual",
    )(a, w, res)


def _merge_kernel(a0_ref, a1_ref, a2_ref, w0_ref, w1_ref, w2_ref,
                  g0_ref, g1_ref, g2_ref, o_ref):
    f32 = jnp.float32
    y = _sigmoid(g0_ref[...].astype(f32)) * _bdot(a0_ref[...], w0_ref[...])
    y = y + _sigmoid(g1_ref[...].astype(f32)) * _bdot(a1_ref[...], w1_ref[...])
    y = y + _sigmoid(g2_ref[...].astype(f32)) * _bdot(a2_ref[...], w2_ref[...])
    o_ref[...] = y.astype(o_ref.dtype)


def branch_merge(o_dn, o_hg, o_gl, w_branch, l, gates, tm, tn):
    m = o_dn.shape[0]
    n = D_MODEL
    per_branch = D_MODEL // tn
    hg_blk = DN_V // HG_V
    gl_blk = (DN_V + HG_V) // GLA_V

    def gate_spec(b):
        return pl.BlockSpec((tm, tn), lambda i, j: (i, b * per_branch + j))

    return pl.pallas_call(
        _merge_kernel,
        grid=(m // tm, n // tn),
        in_specs=[_resident((tm, DN_V), lambda i, j: (i, 0)),
                  _resident((tm, HG_V), lambda i, j: (i, 0)),
                  _resident((tm, GLA_V), lambda i, j: (i, 0)),
                  pl.BlockSpec((None, DN_V, tn), lambda i, j: (l, 0, j)),
                  pl.BlockSpec((None, HG_V, tn), lambda i, j: (l, hg_blk, j)),
                  pl.BlockSpec((None, GLA_V, tn), lambda i, j: (l, gl_blk, j)),
                  gate_spec(0), gate_spec(1), gate_spec(2)],
        out_specs=pl.BlockSpec((tm, tn), lambda i, j: (i, j)),
        out_shape=jax.ShapeDtypeStruct((m, n), BF16),
        compiler_params=_cp(("parallel", "arbitrary")),
        name="branch_merge",
    )(o_dn, o_hg, o_gl, w_branch, w_branch, w_branch, gates, gates, gates)


def _iota2(shape, dim):
    return lax.broadcasted_iota(jnp.int32, shape, dim)


def _chunk_cumsum_matrix(tb, lower):
    r = _iota2((tb, tb), 0)
    c = _iota2((tb, tb), 1)
    same = (r // CHUNK) == (c // CHUNK)
    tri = (c <= r) if lower else (r <= c)
    return jnp.where(same & tri, 1.0, 0.0).astype(jnp.float32)


def _head_out(o, nw, gate):
    y = o * lax.rsqrt(jnp.mean(o * o, axis=-1, keepdims=True) + EPS) * nw
    return y * (gate * _sigmoid(gate))


def _chunks(x, nc):
    return [x[c * CHUNK:(c + 1) * CHUNK] for c in range(nc)]


def _unit_lower_inverse(a_strict):
    r = _iota2((CHUNK, CHUNK), 0)
    c = _iota2((CHUNK, CHUNK), 1)
    eye = jnp.where(r == c, 1.0, 0.0).astype(jnp.float32)
    blk = (r // SUB) == (c // SUB)
    d = jnp.where(blk, a_strict, 0.0)
    e = a_strict - d
    d2 = _bmm(d, d)
    d4 = _bmm(d2, d2)
    d8 = _bmm(d4, d4)
    p = eye - d
    p = p + _bmm(p, d2)
    p = p + _bmm(p, d4)
    p = p + _bmm(p, d8)
    n = _bmm(p, e)
    n2 = _bmm(n, n)
    q = eye - n + n2 - _bmm(n, n2)
    return q, p


def _dn_kernel(q_ref, k_ref, v_ref, z_ref, sm_ref, cq_ref, ck_ref, cv_ref,
               alog_ref, dtb_ref, nw_ref, o_ref, s_ref, carry_ref, *, tb):
    g = pl.program_id(0)
    t = pl.program_id(1)

    @pl.when(t == 0)
    def _():
        s_ref[...] = jnp.zeros_like(s_ref)
        carry_ref[...] = jnp.zeros_like(carry_ref)

    nc = tb // CHUNK
    hb = DN_HB

    def conv_silu(x_ref, cw_ref, idx):
        x = x_ref[...]
        ext = jnp.concatenate([carry_ref[idx], x], axis=0)
        w = cw_ref[...]
        y = x * w[DN_CONV - 1:DN_CONV]
        for s in range(1, DN_CONV):
            y = y + pltpu.roll(ext, s, 0)[8:] * w[DN_CONV - 1 - s:DN_CONV - s]
        carry_ref[idx] = x[tb - 8:]
        return y * _sigmoid(y)

    yq = conv_silu(q_ref, cq_ref, 0)
    yk = conv_silu(k_ref, ck_ref, 1)
    yv = conv_silu(v_ref, cv_ref, 2)

    sm = sm_ref[...]
    sm = jnp.where(g == 0, sm, pltpu.roll(sm, LANES - hb, 1))
    lane = _iota2((1, LANES), 1)
    is_alpha = (lane >= ALPHA_LANE) & (lane < ALPHA_LANE + hb)
    a_coef = jnp.where(is_alpha, jnp.exp(alog_ref[0]), 0.0)
    beta_slab = _sigmoid(sm)
    la_slab = -a_coef * _softplus(sm + dtb_ref[0])
    bcum_slab = _dot(_chunk_cumsum_matrix(tb, True), la_slab, HI)
    la_t = la_slab.T[ALPHA_LANE:ALPHA_LANE + 8]
    bcum_t = _dot(la_t, _chunk_cumsum_matrix(tb, False), HI)

    def stack_heads(per_head):
        return jnp.stack([per_head[h][c] for c in range(nc) for h in range(hb)], axis=0)

    qs, ks, vs, betas, bcs = [], [], [], [], []
    for h in range(hb):
        hs = slice(h * DN_DK, (h + 1) * DN_DK)
        qh, kh = yq[:, hs], yk[:, hs]
        qn = qh * lax.rsqrt(jnp.sum(qh * qh, axis=-1, keepdims=True) + EPS) * (DN_DK ** -0.5)
        kn = kh * lax.rsqrt(jnp.sum(kh * kh, axis=-1, keepdims=True) + EPS)
        qs.append(_chunks(qn, nc))
        ks.append(_chunks(kn, nc))
        vs.append(_chunks(yv[:, hs], nc))
        betas.append(_chunks(beta_slab[:, h:h + 1], nc))
        bcs.append(_chunks(bcum_slab[:, ALPHA_LANE + h:ALPHA_LANE + h + 1], nc))
    q3, k3, v3 = stack_heads(qs), stack_heads(ks), stack_heads(vs)
    beta3, bc3 = stack_heads(betas), stack_heads(bcs)
    br3 = jnp.stack([bcum_t[h:h + 1, c * CHUNK:(c + 1) * CHUNK]
                     for c in range(nc) for h in range(hb)], axis=0)

    r64 = _iota2((CHUNK, CHUNK), 0)
    c64 = _iota2((CHUNK, CHUNK), 1)
    causal = c64 <= r64
    strict = c64 < r64
    ebc3 = jnp.exp(bc3)
    kb3 = k3 * beta3
    decay3 = jnp.where(causal, jnp.exp(jnp.where(causal, bc3 - br3, 0.0)), 0.0)
    kq3 = _bmm_nt(jnp.concatenate([kb3, q3], axis=1), k3)
    kk3 = kq3[:, :CHUNK] * decay3
    qk3 = kq3[:, CHUNK:] * decay3
    qmat, pmat = _unit_lower_inverse(jnp.where(strict, kk3, 0.0))
    rhs3 = jnp.concatenate([v3 * beta3, kb3 * ebc3], axis=2)
    sol3 = _bmm(qmat, _bmm(pmat, rhs3))
    u3 = sol3[:, :, :DN_DV]
    wq3 = jnp.concatenate([sol3[:, :, DN_DV:], q3 * ebc3], axis=1)
    b_last3 = bc3[:, CHUNK - 1:CHUNK, :]
    kd3 = k3 * jnp.exp(b_last3 - bc3)
    kdt3 = jnp.stack([kd3[b].T for b in range(nc * hb)], axis=0)
    a_last3 = jnp.exp(b_last3)

    nw = nw_ref[...]
    z_all = z_ref[...]
    state = s_ref[...]
    for c in range(nc):
        bs = slice(c * hb, (c + 1) * hb)
        rs = slice(c * CHUNK, (c + 1) * CHUNK)
        ws = _bmm(wq3[bs], state)
        v_new = u3[bs] - ws[:, :CHUNK]
        o = ws[:, CHUNK:] + _bmm(qk3[bs], v_new)
        state = state * a_last3[bs] + _bmm(kdt3[bs], v_new)
        for h in range(hb):
            hs = slice(h * DN_DK, (h + 1) * DN_DK)
            o_ref[rs, hs] = _head_out(o[h], nw, z_all[rs, hs]).astype(o_ref.dtype)
    s_ref[...] = state


def deltanet(proj_a, small, dn_conv, l, a_log_pad, dtb_pad, dn_norm, tb):
    t = proj_a.shape[0]
    w = DN_HB * DN_DK
    qb, kb_, vb, zb = (A_DN_Q // w, A_DN_K // w, A_DN_V // w, A_DN_Z // w)

    def pspec(base):
        return pl.BlockSpec((tb, w), lambda g, i: (i, base + g))

    def cspec(base):
        return pl.BlockSpec((None, DN_CONV, w), lambda g, i: (l, 0, base + g))

    pad_spec = pl.BlockSpec((1, 1, LANES), lambda g, i: (g, 0, 0))
    assert DN_GROUPS == 2, "the kernel selects its group's lanes with one rotate"
    return pl.pallas_call(
        functools.partial(_dn_kernel, tb=tb),
        grid=(DN_GROUPS, t // tb),
        in_specs=[pspec(qb), pspec(kb_), pspec(vb), pspec(zb),
                  pl.BlockSpec((tb, LANES), lambda g, i: (i, 0)),
                  cspec(qb), cspec(kb_), cspec(vb),
                  pad_spec, pad_spec,
                  pl.BlockSpec((1, DN_DV), lambda g, i: (0, 0))],
        out_specs=pl.BlockSpec((tb, w), lambda g, i: (i, g)),
        out_shape=jax.ShapeDtypeStruct((t, DN_V), BF16),
        scratch_shapes=[pltpu.VMEM((DN_HB, DN_DK, DN_DV), jnp.float32),
                        pltpu.VMEM((3, 8, w), jnp.float32)],
        compiler_params=_cp(("parallel", "arbitrary")),
        name="deltanet",
    )(proj_a, proj_a, proj_a, proj_a, small, dn_conv, dn_conv, dn_conv,
      a_log_pad, dtb_pad, dn_norm.reshape(1, DN_DV))


def _gla_core(q, k, v, g, gate, nw, s_ref, o_ref, tb, hb, dk, dv):
    sub = GLA_SUB
    nsub = tb // sub
    nc = tb // CHUNK
    per = CHUNK // sub
    wk = hb * dk
    b = _dot(_chunk_cumsum_matrix(tb, True), g, HI)

    b3 = b.reshape(nsub, sub, wk)
    last = b3[:, sub - 1:sub, :]
    prev = jnp.concatenate([jnp.zeros((1, 1, wk), jnp.float32), last[:nsub - 1]], axis=0)
    blk_id = _iota2((nsub, 1, wk), 0)
    base3 = jnp.where(blk_id % per == 0, 0.0, prev)
    base = jnp.broadcast_to(base3, (nsub, sub, wk)).reshape(tb, wk)
    qt = q * jnp.exp(b - base)
    qe = q * jnp.exp(b)

    q3 = q.reshape(nsub, sub, wk)
    k3 = k.reshape(nsub, sub, wk)
    row = _iota2((tb, CHUNK), 0)
    col = _iota2((tb, CHUNK), 1)
    same_blk = ((row % CHUNK) // sub) == (col // sub)
    diag = [jnp.zeros((tb, CHUNK), jnp.float32) for _ in range(hb)]
    for j in range(sub):
        bj = b3[:, j:j + 1, :]
        kj = k3[:, j:j + 1, :]
        e = jnp.exp(jnp.minimum(b3 - bj, 0.0))
        prod = (q3 * kj * e).reshape(tb, wk)
        sel = same_blk & ((col % sub) == j) & ((row % sub) >= j)
        for h in range(hb):
            colv = jnp.sum(prod[:, h * dk:(h + 1) * dk], axis=-1, keepdims=True)
            diag[h] = jnp.where(sel, colv, diag[h])

    zeros = lambda r: jnp.zeros((r, dk), jnp.float32)
    lhs_l, rhs_l, tr_l, v_l, qe_l, diag_l = [], [], [], [], [], []
    for c in range(nc):
        rs = slice(c * CHUNK, (c + 1) * CHUNK)
        for h in range(hb):
            ks_ = slice(h * dk, (h + 1) * dk)
            b_c = b[rs, ks_]
            k_c = k[rs, ks_]
            qt_c = qt[rs, ks_]
            lhs, rhs = [], []
            for i in range(1, per):
                lo, hi = i * sub, (i + 1) * sub
                n_i = b_c[lo - 1:lo]
                kt = k_c[:lo] * jnp.exp(n_i - b_c[:lo])
                rhs.append(jnp.concatenate([kt, zeros(CHUNK - lo)], axis=0))
                blk = [zeros(lo), qt_c[lo:hi]] + ([zeros(CHUNK - hi)] if hi < CHUNK else [])
                lhs.append(jnp.concatenate(blk, axis=0))
            lhs_l.append(jnp.concatenate(lhs, axis=1))
            rhs_l.append(jnp.concatenate(rhs, axis=1))
            b_last = b_c[CHUNK - 1:CHUNK]
            kg = k_c * jnp.exp(b_last - b_c)
            e_rows = jnp.broadcast_to(jnp.exp(b_last), (CHUNK, dk))
            tr_l.append(jnp.concatenate([kg, e_rows], axis=0).T)
            v_l.append(v[rs, h * dv:(h + 1) * dv])
            qe_l.append(qe[rs, ks_])
            diag_l.append(diag[h][rs])
    off = _bmm_nt(jnp.stack(lhs_l, axis=0), jnp.stack(rhs_l, axis=0))
    scores = jnp.stack(diag_l, axis=0) + off
    tr3 = jnp.stack(tr_l, axis=0)
    v3 = jnp.stack(v_l, axis=0)
    upd = _bmm(tr3[:, :, :CHUNK], v3)
    state = s_ref[...]
    states = []
    for c in range(nc):
        bs = slice(c * hb, (c + 1) * hb)
        states.append(state)
        state = state * tr3[bs, :, CHUNK:CHUNK + 1] + upd[bs]
    s_ref[...] = state
    o3 = _bmm(jnp.concatenate([jnp.stack(qe_l, axis=0), scores], axis=2),
              jnp.concatenate([jnp.concatenate(states, axis=0), v3], axis=1))
    for c in range(nc):
        rs = slice(c * CHUNK, (c + 1) * CHUNK)
        for h in range(hb):
            vs_ = slice(h * dv, (h + 1) * dv)
            o_ref[rs, vs_] = _head_out(o3[c * hb + h], nw, gate[rs, vs_]).astype(o_ref.dtype)


def _hg_kernel(q_ref, f_ref, i_ref, g_ref, lb_ref, nw_ref, o_ref, s_ref, *, tb):
    @pl.when(pl.program_id(1) == 0)
    def _():
        s_ref[...] = jnp.zeros_like(s_ref)

    lb = lb_ref[...]
    xf = f_ref[...]
    a = jnp.log(lb)
    cc = jnp.log1p(-lb) + _log_sigmoid(xf)
    log_f = jnp.maximum(a, cc) + jnp.log1p(jnp.exp(-jnp.abs(a - cc)))
    k_in = (1.0 - lb) * _sigmoid(-xf)
    q = q_ref[...] * (HG_DK ** -0.5)
    _gla_core(q, k_in, i_ref[...], log_f, g_ref[...], nw_ref[...], s_ref, o_ref, tb,
              HG_HB, HG_DK, HG_DV)


def hgrn2(proj_b, lb, hg_norm, tb):
    t = proj_b.shape[0]
    w = HG_HB * HG_DK

    def pspec(off):
        return pl.BlockSpec((tb, w), lambda g, i: (i, off // w + g))

    return pl.pallas_call(
        functools.partial(_hg_kernel, tb=tb),
        grid=(HG_HEADS // HG_HB, t // tb),
        in_specs=[pspec(B_HG_Q), pspec(B_HG_F), pspec(B_HG_I), pspec(B_HG_G),
                  pl.BlockSpec((1, w), lambda g, i: (0, g)),
                  pl.BlockSpec((1, HG_DV), lambda g, i: (0, 0))],
        out_specs=pl.BlockSpec((tb, w), lambda g, i: (i, g)),
        out_shape=jax.ShapeDtypeStruct((t, HG_V), BF16),
        scratch_shapes=[pltpu.VMEM((HG_HB, HG_DK, HG_DV), jnp.float32)],
        compiler_params=_cp(("parallel", "arbitrary")),
        name="hgrn2",
    )(proj_b, proj_b, proj_b, proj_b, lb.reshape(1, HG_K), hg_norm.reshape(1, HG_DV))


def _gl_kernel(q_ref, k_ref, v_ref, g_ref, sm_ref, w2_ref, bgk_ref, nw_ref, o_ref, s_ref, *, tb):
    @pl.when(pl.program_id(1) == 0)
    def _():
        s_ref[...] = jnp.zeros_like(s_ref)

    gk = _dot(sm_ref[...], w2_ref[...], HI) + bgk_ref[...]
    log_g = _log_sigmoid(gk) / GLA_GATE_NORM
    q = q_ref[...] * (GLA_DK ** -0.5)
    _gla_core(q, k_ref[...], v_ref[...], log_g, g_ref[...], nw_ref[...], s_ref, o_ref, tb,
              GLA_HB, GLA_DK, GLA_DV)


def gla(proj_b, small, w2_pad, b_gk, gla_norm, tb):
    t = proj_b.shape[0]
    wk = GLA_HB * GLA_DK
    wv = GLA_HB * GLA_DV
    return pl.pallas_call(
        functools.partial(_gl_kernel, tb=tb),
        grid=(GLA_HEADS // GLA_HB, t // tb),
        in_specs=[pl.BlockSpec((tb, wk), lambda g, i: (i, B_GL_Q // wk + g)),
                  pl.BlockSpec((tb, wk), lambda g, i: (i, B_GL_K // wk + g)),
                  pl.BlockSpec((tb, wv), lambda g, i: (i, B_GL_V // wv + g)),
                  pl.BlockSpec((tb, wv), lambda g, i: (i, B_GL_G // wv + g)),
                  pl.BlockSpec((tb, LANES), lambda g, i: (i, 0)),
                  pl.BlockSpec((LANES, wk), lambda g, i: (0, g)),
                  pl.BlockSpec((1, wk), lambda g, i: (0, g)),
                  pl.BlockSpec((1, GLA_DV), lambda g, i: (0, 0))],
        out_specs=pl.BlockSpec((tb, wv), lambda g, i: (i, g)),
        out_shape=jax.ShapeDtypeStruct((t, GLA_V), BF16),
        scratch_shapes=[pltpu.VMEM((GLA_HB, GLA_DK, GLA_DV), jnp.float32)],
        compiler_params=_cp(("parallel", "arbitrary")),
        name="gla",
    )(proj_b, proj_b, proj_b, proj_b, small, w2_pad, b_gk.reshape(1, GLA_K), gla_norm.reshape(1, GLA_DV))


def _pad_group_lanes(v):
    vg = v.reshape(DN_GROUPS, 1, DN_HB).astype(jnp.float32)
    return jnp.pad(vg, ((0, 0), (0, 0), (ALPHA_LANE, LANES - ALPHA_LANE - DN_HB)))


FFN_OUT_VMEM = 60 * 1024 * 1024


def _ffn(h, norm_w, w_in, w_out, l):
    u = rmsnorm(h, norm_w, BF16)
    a = swiglu_in(u, w_in, l, tm=2048, tn=256)
    return matmul_residual(a, w_out, l, h, 0.5, tm=1024, tn=256, ksplit=2, vmem=FFN_OUT_VMEM)


def _mixer(h, l, norm_w, w_t, dn_conv, dn_a_log, dn_dt_bias, dn_norm, lb, hg_norm,
           gla_w_gk2, gla_b_gk, gla_norm, w_branch, w_out):
    u = rmsnorm(h, norm_w, BF16)
    proj_a = matmul_nt(u, w_t, l, 0, SRC_DN_B, jnp.float32, tm=2048, tn=512)
    proj_b = matmul_nt(u, w_t, l, SRC_HG, B_COLS, jnp.float32, tm=2048, tn=512)
    gates = matmul_nt(u, w_t, l, SRC_GATES, N_BRANCH * D_MODEL, BF16, tm=2048, tn=512)
    small = small_proj(u, w_t, l, tm=1024)
    o_dn = deltanet(proj_a, small, dn_conv, l, _pad_group_lanes(dn_a_log), _pad_group_lanes(dn_dt_bias),
                    dn_norm, DN_TB)
    o_hg = hgrn2(proj_b, lb, hg_norm, GLA_TB)
    w2_pad = jnp.pad(gla_w_gk2.astype(jnp.float32), ((GK_LANE, LANES - GK_LANE - GLA_RANK), (0, 0)))
    o_gl = gla(proj_b, small, w2_pad, gla_b_gk, gla_norm, GLA_TB)
    y = branch_merge(o_dn, o_hg, o_gl, w_branch, l, gates, tm=2048, tn=256)
    return matmul_residual(y, w_out, l, h, 1.0, tm=2048, tn=256)


def kernel(x, norm_ffn1, ffn1_w_in, ffn1_w_out, norm_mix, w_in, dn_conv, dn_a_log, dn_dt_bias,
           dn_norm, hg_lower_bounds, hg_norm, gla_w_gk2, gla_b_gk, gla_norm, w_branch, w_out,
           norm_ffn2, ffn2_w_in, ffn2_w_out, norm_final):
    bsz, t, d = x.shape
    assert bsz == 1, "the chunked recurrences run over one sequence"
    lb_all = jnp.cumsum(jax.nn.softmax(hg_lower_bounds.astype(jnp.float32), axis=0), axis=0)
    lb_all = lb_all - lb_all[0]
    w_t = jnp.swapaxes(w_in, 1, 2)
    h = x.reshape(t, d)
    for l in range(DEPTH):
        h = _ffn(h, norm_ffn1[l], ffn1_w_in, ffn1_w_out, l)
        h = _mixer(h, l, norm_mix[l], w_t, dn_conv, dn_a_log[l], dn_dt_bias[l],
                   dn_norm[l], lb_all[l], hg_norm[l], gla_w_gk2[l], gla_b_gk[l], gla_norm[l],
                   w_branch, w_out)
        h = _ffn(h, norm_ffn2[l], ffn2_w_in, ffn2_w_out, l)
    return rmsnorm(h, norm_final, x.dtype).reshape(bsz, t, d)
```

```python
import functools

import jax
import jax.numpy as jnp
from jax import lax
from jax.experimental import pallas as pl
from jax.experimental.pallas import tpu as pltpu

D_MODEL = 4096
DEPTH = 2
CHUNK = 64
SUB = 16
GLA_SUB = 8
EPS = 1e-6
DN_HEADS, DN_DK, DN_DV, DN_CONV = 16, 128, 128, 4
HG_HEADS, HG_DK, HG_DV = 8, 128, 128
GLA_HEADS, GLA_DK, GLA_DV, GLA_RANK = 4, 128, 256, 16
GLA_GATE_NORM = 16.0
D_FF = 11008
N_BRANCH = 3

DN_QK = DN_HEADS * DN_DK
DN_V = DN_HEADS * DN_DV
HG_K = HG_HEADS * HG_DK
HG_V = HG_HEADS * HG_DV
GLA_K = GLA_HEADS * GLA_DK
GLA_V = GLA_HEADS * GLA_DV

SRC_DN_B = 2 * DN_QK + 2 * DN_V
SRC_HG = SRC_DN_B + 2 * DN_HEADS
SRC_GK = SRC_HG + 2 * HG_K + 2 * HG_V + 2 * GLA_K + 2 * GLA_V
SRC_GATES = SRC_GK + GLA_RANK
A_DN_Q, A_DN_K, A_DN_V, A_DN_Z = 0, 2048, 4096, 6144
B_HG_Q, B_HG_F, B_HG_I, B_HG_G = 0, 1024, 2048, 3072
B_GL_Q, B_GL_K, B_GL_V, B_GL_G = 4096, 4608, 5120, 6144
B_COLS = SRC_GK - SRC_HG

DN_HB = 8
DN_GROUPS = DN_HEADS // DN_HB
HG_HB = 8
GLA_HB = 4
DN_TB = 256
GLA_TB = 256
LANES = 128
ALPHA_LANE = DN_HEADS
GK_LANE = 2 * DN_HEADS

VMEM_LIMIT = 56 * 1024 * 1024
HI = lax.Precision.HIGHEST
BF16 = jnp.bfloat16


def _cp(sem, vmem=VMEM_LIMIT):
    return pltpu.CompilerParams(dimension_semantics=sem, vmem_limit_bytes=vmem)


def _bf(x):
    return x if x.dtype == BF16 else x.astype(BF16)


def _dot(a, b, precision=None):
    return jnp.dot(a, b, preferred_element_type=jnp.float32, precision=precision)


def _bdot(a, b):
    return jnp.dot(_bf(a), _bf(b), preferred_element_type=jnp.float32)


def _bdot_nt(a, b):
    return lax.dot_general(_bf(a), _bf(b), (((1,), (1,)), ((), ())),
                           preferred_element_type=jnp.float32)


def _bmm(a, b):
    return jnp.einsum("bmk,bkn->bmn", _bf(a), _bf(b), preferred_element_type=jnp.float32)


def _bmm_nt(a, b):
    return jnp.einsum("bmk,bnk->bmn", _bf(a), _bf(b), preferred_element_type=jnp.float32)


def _sigmoid(x):
    return jax.nn.sigmoid(x)


def _softplus(x):
    return jnp.maximum(x, 0.0) + jnp.log(1.0 + jnp.exp(-jnp.abs(x)))


def _log_sigmoid(x):
    return jnp.minimum(x, 0.0) - jnp.log(1.0 + jnp.exp(-jnp.abs(x)))


def _resident(shape, index_map):
    return pl.BlockSpec(shape, index_map, pipeline_mode=pl.Buffered(1))


def _rmsnorm_kernel(x_ref, w_ref, o_ref):
    x = x_ref[...]
    y = x * lax.rsqrt(jnp.mean(x * x, axis=-1, keepdims=True) + EPS)
    o_ref[...] = (y * w_ref[...]).astype(o_ref.dtype)


def rmsnorm(x, w, out_dtype, tr=256):
    t, d = x.shape
    return pl.pallas_call(
        _rmsnorm_kernel,
        grid=(t // tr,),
        in_specs=[pl.BlockSpec((tr, d), lambda i: (i, 0)),
                  pl.BlockSpec((1, d), lambda i: (0, 0))],
        out_specs=pl.BlockSpec((tr, d), lambda i: (i, 0)),
        out_shape=jax.ShapeDtypeStruct((t, d), out_dtype),
        compiler_params=_cp(("parallel",)),
        name="rmsnorm",
    )(x, w.reshape(1, d))


def _mm_nt_kernel(a_ref, w_ref, o_ref):
    o_ref[...] = _bdot_nt(a_ref[...], w_ref[0]).astype(o_ref.dtype)


def _rows_spec(l, row0, rows, k, sublane=8):
    assert row0 % sublane == 0 and rows % sublane == 0
    return pl.BlockSpec((pl.Element(1), pl.Element(rows), pl.Element(k)),
                        lambda i, j: (l, pl.multiple_of(row0 + j * rows, sublane), 0))


def matmul_nt(a, w_t, l, row0, n, out_dtype, tm, tn):
    m, k = a.shape
    assert n % tn == 0
    return pl.pallas_call(
        _mm_nt_kernel,
        grid=(m // tm, n // tn),
        in_specs=[_resident((tm, k), lambda i, j: (i, 0)),
                  _rows_spec(l, row0, tn, k)],
        out_specs=pl.BlockSpec((tm, tn), lambda i, j: (i, j)),
        out_shape=jax.ShapeDtypeStruct((m, n), out_dtype),
        compiler_params=_cp(("parallel", "arbitrary")),
        name="matmul_nt",
    )(a, w_t)


def _small_kernel(a_ref, wba_ref, wgk_ref, o_ref):
    k = a_ref.shape[1]
    pad = jnp.zeros((LANES - GK_LANE - GLA_RANK, k), BF16)
    w = jnp.concatenate([_bf(wba_ref[0]), _bf(wgk_ref[0]), pad], axis=0)
    o_ref[...] = _bdot_nt(a_ref[...], w)


def small_proj(a, w_t, l, tm):
    m, k = a.shape
    return pl.pallas_call(
        _small_kernel,
        grid=(m // tm, 1),
        in_specs=[pl.BlockSpec((tm, k), lambda i, j: (i, 0)),
                  _rows_spec(l, SRC_DN_B, 2 * DN_HEADS, k),
                  _rows_spec(l, SRC_GK, GLA_RANK, k)],
        out_specs=pl.BlockSpec((tm, LANES), lambda i, j: (i, 0)),
        out_shape=jax.ShapeDtypeStruct((m, LANES), jnp.float32),
        compiler_params=_cp(("parallel", "arbitrary")),
        name="small_proj",
    )(a, w_t, w_t)


def _swiglu_kernel(a_ref, wg_ref, wu_ref, o_ref, *, tn):
    w = jnp.concatenate([_bf(wg_ref[...]), _bf(wu_ref[...])], axis=1)
    r = _bdot(a_ref[...], w)
    g = r[:, :tn]
    u = r[:, tn:]
    o_ref[...] = (g * _sigmoid(g) * u).astype(o_ref.dtype)


def swiglu_in(a, w_in, l, tm, tn):
    m, k = a.shape
    nf = w_in.shape[2] // 2
    nj = nf // tn
    return pl.pallas_call(
        functools.partial(_swiglu_kernel, tn=tn),
        grid=(m // tm, nj),
        in_specs=[_resident((tm, k), lambda i, j: (i, 0)),
                  pl.BlockSpec((None, k, tn), lambda i, j: (l, 0, j)),
                  pl.BlockSpec((None, k, tn), lambda i, j: (l, 0, nj + j))],
        out_specs=pl.BlockSpec((tm, tn), lambda i, j: (i, j)),
        out_shape=jax.ShapeDtypeStruct((m, nf), BF16),
        compiler_params=_cp(("parallel", "arbitrary")),
        name="swiglu_in",
    )(a, w_in, w_in)


def _mm_res_kernel(a_ref, w_ref, r_ref, o_ref, *, scale, ksplit):
    k = a_ref.shape[1]
    kc = k // ksplit
    acc = _bdot(a_ref[:, :kc], w_ref[:kc, :])
    for s in range(1, ksplit):
        acc = acc + _bdot(a_ref[:, s * kc:(s + 1) * kc], w_ref[s * kc:(s + 1) * kc, :])
    o_ref[...] = r_ref[...] + scale * acc


def matmul_residual(a, w, l, res, scale, tm, tn, ksplit=1, vmem=VMEM_LIMIT):
    m, k = a.shape
    n = w.shape[2]
    return pl.pallas_call(
        functools.partial(_mm_res_kernel, scale=scale, ksplit=ksplit),
        grid=(m // tm, n // tn),
        in_specs=[_resident((tm, k), lambda i, j: (i, 0)),
                  pl.BlockSpec((None, k, tn), lambda i, j: (l, 0, j)),
                  pl.BlockSpec((tm, tn), lambda i, j: (i, j))],
        out_specs=pl.BlockSpec((tm, tn), lambda i, j: (i, j)),
        out_shape=jax.ShapeDtypeStruct((m, n), jnp.float32),
        compiler_params=_cp(("parallel", "arbitrary"), vmem),
        name="matmul_residual",
    )(a, w, res)


def _merge_kernel(a0_ref, a1_ref, a2_ref, w0_ref, w1_ref, w2_ref,
                  g0_ref, g1_ref, g2_ref, o_ref):
    f32 = jnp.float32
    y = _sigmoid(g0_ref[...].astype(f32)) * _bdot(a0_ref[...], w0_ref[...])
    y = y + _sigmoid(g1_ref[...].astype(f32)) * _bdot(a1_ref[...], w1_ref[...])
    y = y + _sigmoid(g2_ref[...].astype(f32)) * _bdot(a2_ref[...], w2_ref[...])
    o_ref[...] = y.astype(o_ref.dtype)


def branch_merge(o_dn, o_hg, o_gl, w_branch, l, gates, tm, tn):
    m = o_dn.shape[0]
    n = D_MODEL
    per_branch = D_MODEL // tn
    hg_blk = DN_V // HG_V
    gl_blk = (DN_V + HG_V) // GLA_V

    def gate_spec(b):
        return pl.BlockSpec((tm, tn), lambda i, j: (i, b * per_branch + j))

    return pl.pallas_call(
        _merge_kernel,
        grid=(m // tm, n // tn),
        in_specs=[_resident((tm, DN_V), lambda i, j: (i, 0)),
                  _resident((tm, HG_V), lambda i, j: (i, 0)),
                  _resident((tm, GLA_V), lambda i, j: (i, 0)),
                  pl.BlockSpec((None, DN_V, tn), lambda i, j: (l, 0, j)),
                  pl.BlockSpec((None, HG_V, tn), lambda i, j: (l, hg_blk, j)),
                  pl.BlockSpec((None, GLA_V, tn), lambda i, j: (l, gl_blk, j)),
                  gate_spec(0), gate_spec(1), gate_spec(2)],
        out_specs=pl.BlockSpec((tm, tn), lambda i, j: (i, j)),
        out_shape=jax.ShapeDtypeStruct((m, n), BF16),
        compiler_params=_cp(("parallel", "arbitrary")),
        name="branch_merge",
    )(o_dn, o_hg, o_gl, w_branch, w_branch, w_branch, gates, gates, gates)


def _iota2(shape, dim):
    return lax.broadcasted_iota(jnp.int32, shape, dim)


def _chunk_cumsum_matrix(tb, lower):
    r = _iota2((tb, tb), 0)
    c = _iota2((tb, tb), 1)
    same = (r // CHUNK) == (c // CHUNK)
    tri = (c <= r) if lower else (r <= c)
    return jnp.where(same & tri, 1.0, 0.0).astype(jnp.float32)


def _head_out(o, nw, gate):
    y = o * lax.rsqrt(jnp.mean(o * o, axis=-1, keepdims=True) + EPS) * nw
    return y * (gate * _sigmoid(gate))


def _chunks(x, nc):
    return [x[c * CHUNK:(c + 1) * CHUNK] for c in range(nc)]


def _unit_lower_inverse(a_strict):
    r = _iota2((CHUNK, CHUNK), 0)
    c = _iota2((CHUNK, CHUNK), 1)
    eye = jnp.where(r == c, 1.0, 0.0).astype(jnp.float32)
    blk = (r // SUB) == (c // SUB)
    d = jnp.where(blk, a_strict, 0.0)
    e = a_strict - d
    d2 = _bmm(d, d)
    d4 = _bmm(d2, d2)
    d8 = _bmm(d4, d4)
    p = eye - d
    p = p + _bmm(p, d2)
    p = p + _bmm(p, d4)
    p = p + _bmm(p, d8)
    n = _bmm(p, e)
    n2 = _bmm(n, n)
    q = eye - n + n2 - _bmm(n, n2)
    return q, p


def _dn_kernel(q_ref, k_ref, v_ref, z_ref, sm_ref, cq_ref, ck_ref, cv_ref,
               alog_ref, dtb_ref, nw_ref, o_ref, s_ref, carry_ref, *, tb):
    g = pl.program_id(0)
    t = pl.program_id(1)

    @pl.when(t == 0)
    def _():
        s_ref[...] = jnp.zeros_like(s_ref)
        carry_ref[...] = jnp.zeros_like(carry_ref)

    nc = tb // CHUNK
    hb = DN_HB

    def conv_silu(x_ref, cw_ref, idx):
        x = x_ref[...]
        ext = jnp.concatenate([carry_ref[idx], x], axis=0)
        w = cw_ref[...]
        y = x * w[DN_CONV - 1:DN_CONV]
        for s in range(1, DN_CONV):
            y = y + pltpu.roll(ext, s, 0)[8:] * w[DN_CONV - 1 - s:DN_CONV - s]
        carry_ref[idx] = x[tb - 8:]
        return y * _sigmoid(y)

    yq = conv_silu(q_ref, cq_ref, 0)
    yk = conv_silu(k_ref, ck_ref, 1)
    yv = conv_silu(v_ref, cv_ref, 2)

    sm = sm_ref[...]
    sm = jnp.where(g == 0, sm, pltpu.roll(sm, LANES - hb, 1))
    lane = _iota2((1, LANES), 1)
    is_alpha = (lane >= ALPHA_LANE) & (lane < ALPHA_LANE + hb)
    a_coef = jnp.where(is_alpha, jnp.exp(alog_ref[0]), 0.0)
    beta_slab = _sigmoid(sm)
    la_slab = -a_coef * _softplus(sm + dtb_ref[0])
    bcum_slab = _dot(_chunk_cumsum_matrix(tb, True), la_slab, HI)
    la_t = la_slab.T[ALPHA_LANE:ALPHA_LANE + 8]
    bcum_t = _dot(la_t, _chunk_cumsum_matrix(tb, False), HI)

    def stack_heads(per_head):
        return jnp.stack([per_head[h][c] for c in range(nc) for h in range(hb)], axis=0)

    qs, ks, vs, betas, bcs = [], [], [], [], []
    for h in range(hb):
        hs = slice(h * DN_DK, (h + 1) * DN_DK)
        qh, kh = yq[:, hs], yk[:, hs]
        qn = qh * lax.rsqrt(jnp.sum(qh * qh, axis=-1, keepdims=True) + EPS) * (DN_DK ** -0.5)
        kn = kh * lax.rsqrt(jnp.sum(kh * kh, axis=-1, keepdims=True) + EPS)
        qs.append(_chunks(qn, nc))
        ks.append(_chunks(kn, nc))
        vs.append(_chunks(yv[:, hs], nc))
        betas.append(_chunks(beta_slab[:, h:h + 1], nc))
        bcs.append(_chunks(bcum_slab[:, ALPHA_LANE + h:ALPHA_LANE + h + 1], nc))
    q3, k3, v3 = stack_heads(qs), stack_heads(ks), stack_heads(vs)
    beta3, bc3 = stack_heads(betas), stack_heads(bcs)
    br3 = jnp.stack([bcum_t[h:h + 1, c * CHUNK:(c + 1) * CHUNK]
                     for c in range(nc) for h in range(hb)], axis=0)

    r64 = _iota2((CHUNK, CHUNK), 0)
    c64 = _iota2((CHUNK, CHUNK), 1)
    causal = c64 <= r64
    strict = c64 < r64
    ebc3 = jnp.exp(bc3)
    kb3 = k3 * beta3
    decay3 = jnp.where(causal, jnp.exp(jnp.where(causal, bc3 - br3, 0.0)), 0.0)
    kq3 = _bmm_nt(jnp.concatenate([kb3, q3], axis=1), k3)
    kk3 = kq3[:, :CHUNK] * decay3
    qk3 = kq3[:, CHUNK:] * decay3
    qmat, pmat = _unit_lower_inverse(jnp.where(strict, kk3, 0.0))
    rhs3 = jnp.concatenate([v3 * beta3, kb3 * ebc3], axis=2)
    sol3 = _bmm(qmat, _bmm(pmat, rhs3))
    u3 = sol3[:, :, :DN_DV]
    wq3 = jnp.concatenate([sol3[:, :, DN_DV:], q3 * ebc3], axis=1)
    b_last3 = bc3[:, CHUNK - 1:CHUNK, :]
    kd3 = k3 * jnp.exp(b_last3 - bc3)
    kdt3 = jnp.stack([kd3[b].T for b in range(nc * hb)], axis=0)
    a_last3 = jnp.exp(b_last3)

    nw = nw_ref[...]
    z_all = z_ref[...]
    state = s_ref[...]
    for c in range(nc):
        bs = slice(c * hb, (c + 1) * hb)
        rs = slice(c * CHUNK, (c + 1) * CHUNK)
        ws = _bmm(wq3[bs], state)
        v_new = u3[bs] - ws[:, :CHUNK]
        o = ws[:, CHUNK:] + _bmm(qk3[bs], v_new)
        state = state * a_last3[bs] + _bmm(kdt3[bs], v_new)
        for h in range(hb):
            hs = slice(h * DN_DK, (h + 1) * DN_DK)
            o_ref[rs, hs] = _head_out(o[h], nw, z_all[rs, hs]).astype(o_ref.dtype)
    s_ref[...] = state


def deltanet(proj_a, small, dn_conv, l, a_log_pad, dtb_pad, dn_norm, tb):
    t = proj_a.shape[0]
    w = DN_HB * DN_DK
    qb, kb_, vb, zb = (A_DN_Q // w, A_DN_K // w, A_DN_V // w, A_DN_Z // w)

    def pspec(base):
        return pl.BlockSpec((tb, w), lambda g, i: (i, base + g))

    def cspec(base):
        return pl.BlockSpec((None, DN_CONV, w), lambda g, i: (l, 0, base + g))

    pad_spec = pl.BlockSpec((1, 1, LANES), lambda g, i: (g, 0, 0))
    assert DN_GROUPS == 2, "the kernel selects its group's lanes with one rotate"
    return pl.pallas_call(
        functools.partial(_dn_kernel, tb=tb),
        grid=(DN_GROUPS, t // tb),
        in_specs=[pspec(qb), pspec(kb_), pspec(vb), pspec(zb),
                  pl.BlockSpec((tb, LANES), lambda g, i: (i, 0)),
                  cspec(qb), cspec(kb_), cspec(vb),
                  pad_spec, pad_spec,
                  pl.BlockSpec((1, DN_DV), lambda g, i: (0, 0))],
        out_specs=pl.BlockSpec((tb, w), lambda g, i: (i, g)),
        out_shape=jax.ShapeDtypeStruct((t, DN_V), BF16),
        scratch_shapes=[pltpu.VMEM((DN_HB, DN_DK, DN_DV), jnp.float32),
                        pltpu.VMEM((3, 8, w), jnp.float32)],
        compiler_params=_cp(("parallel", "arbitrary")),
        name="deltanet",
    )(proj_a, proj_a, proj_a, proj_a, small, dn_conv, dn_conv, dn_conv,
      a_log_pad, dtb_pad, dn_norm.reshape(1, DN_DV))


def _gla_core(q, k, v, g, gate, nw, s_ref, o_ref, tb, hb, dk, dv):
    sub = GLA_SUB
    nsub = tb // sub
    nc = tb // CHUNK
    per = CHUNK // sub
    wk = hb * dk
    b = _dot(_chunk_cumsum_matrix(tb, True), g, HI)

    b3 = b.reshape(nsub, sub, wk)
    last = b3[:, sub - 1:sub, :]
    prev = jnp.concatenate([jnp.zeros((1, 1, wk), jnp.float32), last[:nsub - 1]], axis=0)
    blk_id = _iota2((nsub, 1, wk), 0)
    base3 = jnp.where(blk_id % per == 0, 0.0, prev)
    base = jnp.broadcast_to(base3, (nsub, sub, wk)).reshape(tb, wk)
    qt = q * jnp.exp(b - base)
    qe = q * jnp.exp(b)

    q3 = q.reshape(nsub, sub, wk)
    k3 = k.reshape(nsub, sub, wk)
    row = _iota2((tb, CHUNK), 0)
    col = _iota2((tb, CHUNK), 1)
    same_blk = ((row % CHUNK) // sub) == (col // sub)
    diag = [jnp.zeros((tb, CHUNK), jnp.float32) for _ in range(hb)]
    for j in range(sub):
        bj = b3[:, j:j + 1, :]
        kj = k3[:, j:j + 1, :]
        e = jnp.exp(jnp.minimum(b3 - bj, 0.0))
        prod = (q3 * kj * e).reshape(tb, wk)
        sel = same_blk & ((col % sub) == j) & ((row % sub) >= j)
        for h in range(hb):
            colv = jnp.sum(prod[:, h * dk:(h + 1) * dk], axis=-1, keepdims=True)
            diag[h] = jnp.where(sel, colv, diag[h])

    zeros = lambda r: jnp.zeros((r, dk), jnp.float32)
    lhs_l, rhs_l, tr_l, v_l, qe_l, diag_l = [], [], [], [], [], []
    for c in range(nc):
        rs = slice(c * CHUNK, (c + 1) * CHUNK)
        for h in range(hb):
            ks_ = slice(h * dk, (h + 1) * dk)
            b_c = b[rs, ks_]
            k_c = k[rs, ks_]
            qt_c = qt[rs, ks_]
            lhs, rhs = [], []
            for i in range(1, per):
                lo, hi = i * sub, (i + 1) * sub
                n_i = b_c[lo - 1:lo]
                kt = k_c[:lo] * jnp.exp(n_i - b_c[:lo])
                rhs.append(jnp.concatenate([kt, zeros(CHUNK - lo)], axis=0))
                blk = [zeros(lo), qt_c[lo:hi]] + ([zeros(CHUNK - hi)] if hi < CHUNK else [])
                lhs.append(jnp.concatenate(blk, axis=0))
            lhs_l.append(jnp.concatenate(lhs, axis=1))
            rhs_l.append(jnp.concatenate(rhs, axis=1))
            b_last = b_c[CHUNK - 1:CHUNK]
            kg = k_c * jnp.exp(b_last - b_c)
            e_rows = jnp.broadcast_to(jnp.exp(b_last), (CHUNK, dk))
            tr_l.append(jnp.concatenate([kg, e_rows], axis=0).T)
            v_l.append(v[rs, h * dv:(h + 1) * dv])
            qe_l.append(qe[rs, ks_])
            diag_l.append(diag[h][rs])
    off = _bmm_nt(jnp.stack(lhs_l, axis=0), jnp.stack(rhs_l, axis=0))
    scores = jnp.stack(diag_l, axis=0) + off
    tr3 = jnp.stack(tr_l, axis=0)
    v3 = jnp.stack(v_l, axis=0)
    upd = _bmm(tr3[:, :, :CHUNK], v3)
    state = s_ref[...]
    states = []
    for c in range(nc):
        bs = slice(c * hb, (c + 1) * hb)
        states.append(state)
        state = state * tr3[bs, :, CHUNK:CHUNK + 1] + upd[bs]
    s_ref[...] = state
    o3 = _bmm(jnp.concatenate([jnp.stack(qe_l, axis=0), scores], axis=2),
              jnp.concatenate([jnp.concatenate(states, axis=0), v3], axis=1))
    for c in range(nc):
        rs = slice(c * CHUNK, (c + 1) * CHUNK)
        for h in range(hb):
            vs_ = slice(h * dv, (h + 1) * dv)
            o_ref[rs, vs_] = _head_out(o3[c * hb + h], nw, gate[rs, vs_]).astype(o_ref.dtype)


def _hg_kernel(q_ref, f_ref, i_ref, g_ref, lb_ref, nw_ref, o_ref, s_ref, *, tb):
    @pl.when(pl.program_id(1) == 0)
    def _():
        s_ref[...] = jnp.zeros_like(s_ref)

    lb = lb_ref[...]
    xf = f_ref[...]
    a = jnp.log(lb)
    cc = jnp.log1p(-lb) + _log_sigmoid(xf)
    log_f = jnp.maximum(a, cc) + jnp.log(1.0 + jnp.exp(-jnp.abs(a - cc)))
    k_in = (1.0 - lb) * _sigmoid(-xf)
    q = q_ref[...] * (HG_DK ** -0.5)
    _gla_core(q, k_in, i_ref[...], log_f, g_ref[...], nw_ref[...], s_ref, o_ref, tb,
              HG_HB, HG_DK, HG_DV)


def hgrn2(proj_b, lb, hg_norm, tb):
    t = proj_b.shape[0]
    w = HG_HB * HG_DK

    def pspec(off):
        return pl.BlockSpec((tb, w), lambda g, i: (i, off // w + g))

    return pl.pallas_call(
        functools.partial(_hg_kernel, tb=tb),
        grid=(HG_HEADS // HG_HB, t // tb),
        in_specs=[pspec(B_HG_Q), pspec(B_HG_F), pspec(B_HG_I), pspec(B_HG_G),
                  pl.BlockSpec((1, w), lambda g, i: (0, g)),
                  pl.BlockSpec((1, HG_DV), lambda g, i: (0, 0))],
        out_specs=pl.BlockSpec((tb, w), lambda g, i: (i, g)),
        out_shape=jax.ShapeDtypeStruct((t, HG_V), BF16),
        scratch_shapes=[pltpu.VMEM((HG_HB, HG_DK, HG_DV), jnp.float32)],
        compiler_params=_cp(("parallel", "arbitrary")),
        name="hgrn2",
    )(proj_b, proj_b, proj_b, proj_b, lb.reshape(1, HG_K), hg_norm.reshape(1, HG_DV))


def _gl_kernel(q_ref, k_ref, v_ref, g_ref, sm_ref, w2_ref, bgk_ref, nw_ref, o_ref, s_ref, *, tb):
    @pl.when(pl.program_id(1) == 0)
    def _():
        s_ref[...] = jnp.zeros_like(s_ref)

    gk = _dot(sm_ref[...], w2_ref[...], HI) + bgk_ref[...]
    log_g = _log_sigmoid(gk) / GLA_GATE_NORM
    q = q_ref[...] * (GLA_DK ** -0.5)
    _gla_core(q, k_ref[...], v_ref[...], log_g, g_ref[...], nw_ref[...], s_ref, o_ref, tb,
              GLA_HB, GLA_DK, GLA_DV)


def gla(proj_b, small, w2_pad, b_gk, gla_norm, tb):
    t = proj_b.shape[0]
    wk = GLA_HB * GLA_DK
    wv = GLA_HB * GLA_DV
    return pl.pallas_call(
        functools.partial(_gl_kernel, tb=tb),
        grid=(GLA_HEADS // GLA_HB, t // tb),
        in_specs=[pl.BlockSpec((tb, wk), lambda g, i: (i, B_GL_Q // wk + g)),
                  pl.BlockSpec((tb, wk), lambda g, i: (i, B_GL_K // wk + g)),
                  pl.BlockSpec((tb, wv), lambda g, i: (i, B_GL_V // wv + g)),
                  pl.BlockSpec((tb, wv), lambda g, i: (i, B_GL_G // wv + g)),
                  pl.BlockSpec((tb, LANES), lambda g, i: (i, 0)),
                  pl.BlockSpec((LANES, wk), lambda g, i: (0, g)),
                  pl.BlockSpec((1, wk), lambda g, i: (0, g)),
                  pl.BlockSpec((1, GLA_DV), lambda g, i: (0, 0))],
        out_specs=pl.BlockSpec((tb, wv), lambda g, i: (i, g)),
        out_shape=jax.ShapeDtypeStruct((t, GLA_V), BF16),
        scratch_shapes=[pltpu.VMEM((GLA_HB, GLA_DK, GLA_DV), jnp.float32)],
        compiler_params=_cp(("parallel", "arbitrary")),
        name="gla",
    )(proj_b, proj_b, proj_b, proj_b, small, w2_pad, b_gk.reshape(1, GLA_K), gla_norm.reshape(1, GLA_DV))


def _pad_group_lanes(v):
    vg = v.reshape(DN_GROUPS, 1, DN_HB).astype(jnp.float32)
    return jnp.pad(vg, ((0, 0), (0, 0), (ALPHA_LANE, LANES - ALPHA_LANE - DN_HB)))


FFN_OUT_VMEM = 60 * 1024 * 1024


def _ffn(h, norm_w, w_in, w_out, l):
    u = rmsnorm(h, norm_w, BF16)
    a = swiglu_in(u, w_in, l, tm=2048, tn=256)
    return matmul_residual(a, w_out, l, h, 0.5, tm=1024, tn=256, ksplit=2, vmem=FFN_OUT_VMEM)


def _mixer(h, l, norm_w, w_t, dn_conv, dn_a_log, dn_dt_bias, dn_norm, lb, hg_norm,
           gla_w_gk2, gla_b_gk, gla_norm, w_branch, w_out):
    u = rmsnorm(h, norm_w, BF16)
    proj_a = matmul_nt(u, w_t, l, 0, SRC_DN_B, jnp.float32, tm=2048, tn=512)
    proj_b = matmul_nt(u, w_t, l, SRC_HG, B_COLS, jnp.float32, tm=2048, tn=512)
    gates = matmul_nt(u, w_t, l, SRC_GATES, N_BRANCH * D_MODEL, BF16, tm=2048, tn=512)
    small = small_proj(u, w_t, l, tm=1024)
    o_dn = deltanet(proj_a, small, dn_conv, l, _pad_group_lanes(dn_a_log), _pad_group_lanes(dn_dt_bias),
                    dn_norm, DN_TB)
    o_hg = hgrn2(proj_b, lb, hg_norm, GLA_TB)
    w2_pad = jnp.pad(gla_w_gk2.astype(jnp.float32), ((GK_LANE, LANES - GK_LANE - GLA_RANK), (0, 0)))
    o_gl = gla(proj_b, small, w2_pad, gla_b_gk, gla_norm, GLA_TB)
    y = branch_merge(o_dn, o_hg, o_gl, w_branch, l, gates, tm=2048, tn=256)
    return matmul_residual(y, w_out, l, h, 1.0, tm=2048, tn=256)


def kernel(x, norm_ffn1, ffn1_w_in, ffn1_w_out, norm_mix, w_in, dn_conv, dn_a_log, dn_dt_bias,
           dn_norm, hg_lower_bounds, hg_norm, gla_w_gk2, gla_b_gk, gla_norm, w_branch, w_out,
           norm_ffn2, ffn2_w_in, ffn2_w_out, norm_final):
    bsz, t, d = x.shape
    assert bsz == 1, "the chunked recurrences run over one sequence"
    lb_all = jnp.cumsum(jax.nn.softmax(hg_lower_bounds.astype(jnp.float32), axis=0), axis=0)
    lb_all = lb_all - lb_all[0]
    w_t = jnp.swapaxes(w_in, 1, 2)
    h = x.reshape(t, d)
    for l in range(DEPTH):
        h = _ffn(h, norm_ffn1[l], ffn1_w_in, ffn1_w_out, l)
        h = _mixer(h, l, norm_mix[l], w_t, dn_conv, dn_a_log[l], dn_dt_bias[l],
                   dn_norm[l], lb_all[l], hg_norm[l], gla_w_gk2[l], gla_b_gk[l], gla_norm[l],
                   w_branch, w_out)
        h = _ffn(h, norm_ffn2[l], ffn2_w_in, ffn2_w_out, l)
    return rmsnorm(h, norm_final, x.dtype).reshape(bsz, t, d)
```

```python
import functools

import jax
import jax.numpy as jnp
from jax import lax
from jax.experimental import pallas as pl
from jax.experimental.pallas import tpu as pltpu

D_MODEL = 4096
DEPTH = 2
CHUNK = 64
SUB = 16
GLA_SUB = 8
EPS = 1e-6
DN_HEADS, DN_DK, DN_DV, DN_CONV = 16, 128, 128, 4
HG_HEADS, HG_DK, HG_DV = 8, 128, 128
GLA_HEADS, GLA_DK, GLA_DV, GLA_RANK = 4, 128, 256, 16
GLA_GATE_NORM = 16.0
D_FF = 11008
N_BRANCH = 3

DN_QK = DN_HEADS * DN_DK
DN_V = DN_HEADS * DN_DV
HG_K = HG_HEADS * HG_DK
HG_V = HG_HEADS * HG_DV
GLA_K = GLA_HEADS * GLA_DK
GLA_V = GLA_HEADS * GLA_DV

SRC_DN_B = 2 * DN_QK + 2 * DN_V
SRC_HG = SRC_DN_B + 2 * DN_HEADS
SRC_GK = SRC_HG + 2 * HG_K + 2 * HG_V + 2 * GLA_K + 2 * GLA_V
SRC_GATES = SRC_GK + GLA_RANK
A_DN_Q, A_DN_K, A_DN_V, A_DN_Z = 0, 2048, 4096, 6144
B_HG_Q, B_HG_F, B_HG_I, B_HG_G = 0, 1024, 2048, 3072
B_GL_Q, B_GL_K, B_GL_V, B_GL_G = 4096, 4608, 5120, 6144
B_COLS = SRC_GK - SRC_HG

DN_HB = 8
DN_GROUPS = DN_HEADS // DN_HB
HG_HB = 8
GLA_HB = 4
DN_TB = 256
GLA_TB = 256
LANES = 128
ALPHA_LANE = DN_HEADS
GK_LANE = 2 * DN_HEADS

VMEM_LIMIT = 56 * 1024 * 1024
WIDE_VMEM = 62 * 1024 * 1024
HI = lax.Precision.HIGHEST
BF16 = jnp.bfloat16


def _cp(sem, vmem=VMEM_LIMIT):
    return pltpu.CompilerParams(dimension_semantics=sem, vmem_limit_bytes=vmem)


def _bf(x):
    return x if x.dtype == BF16 else x.astype(BF16)


def _dot(a, b, precision=None):
    return jnp.dot(a, b, preferred_element_type=jnp.float32, precision=precision)


def _bdot(a, b):
    return jnp.dot(_bf(a), _bf(b), preferred_element_type=jnp.float32)


def _bdot_nt(a, b):
    return lax.dot_general(_bf(a), _bf(b), (((1,), (1,)), ((), ())),
                           preferred_element_type=jnp.float32)


def _bmm(a, b):
    return jnp.einsum("bmk,bkn->bmn", _bf(a), _bf(b), preferred_element_type=jnp.float32)


def _bmm_nt(a, b):
    return jnp.einsum("bmk,bnk->bmn", _bf(a), _bf(b), preferred_element_type=jnp.float32)


def _sigmoid(x):
    return jax.nn.sigmoid(x)


def _softplus(x):
    return jnp.maximum(x, 0.0) + jnp.log(1.0 + jnp.exp(-jnp.abs(x)))


def _log_sigmoid(x):
    return jnp.minimum(x, 0.0) - jnp.log(1.0 + jnp.exp(-jnp.abs(x)))


def _resident(shape, index_map):
    return pl.BlockSpec(shape, index_map, pipeline_mode=pl.Buffered(1))


def _rmsnorm_kernel(x_ref, w_ref, o_ref):
    x = x_ref[...]
    y = x * lax.rsqrt(jnp.mean(x * x, axis=-1, keepdims=True) + EPS)
    o_ref[...] = (y * w_ref[...]).astype(o_ref.dtype)


def rmsnorm(x, w, out_dtype, tr=256):
    t, d = x.shape
    return pl.pallas_call(
        _rmsnorm_kernel,
        grid=(t // tr,),
        in_specs=[pl.BlockSpec((tr, d), lambda i: (i, 0)),
                  pl.BlockSpec((1, d), lambda i: (0, 0))],
        out_specs=pl.BlockSpec((tr, d), lambda i: (i, 0)),
        out_shape=jax.ShapeDtypeStruct((t, d), out_dtype),
        compiler_params=_cp(("parallel",)),
        name="rmsnorm",
    )(x, w.reshape(1, d))


def _mm_nt_kernel(a_ref, w_ref, o_ref):
    o_ref[...] = _bdot_nt(a_ref[...], w_ref[0]).astype(o_ref.dtype)


def _rows_spec(l, row0, rows, k, sublane=8):
    assert row0 % sublane == 0 and rows % sublane == 0
    return pl.BlockSpec((pl.Element(1), pl.Element(rows), pl.Element(k)),
                        lambda i, j: (l, pl.multiple_of(row0 + j * rows, sublane), 0))


def matmul_nt(a, w_t, l, row0, n, out_dtype, tm, tn):
    m, k = a.shape
    assert n % tn == 0
    return pl.pallas_call(
        _mm_nt_kernel,
        grid=(m // tm, n // tn),
        in_specs=[pl.BlockSpec((tm, k), lambda i, j: (i, 0)),
                  _rows_spec(l, row0, tn, k)],
        out_specs=pl.BlockSpec((tm, tn), lambda i, j: (i, j)),
        out_shape=jax.ShapeDtypeStruct((m, n), out_dtype),
        compiler_params=_cp(("parallel", "arbitrary"), WIDE_VMEM),
        name="matmul_nt",
    )(a, w_t)


def _small_kernel(a_ref, wba_ref, wgk_ref, o_ref):
    k = a_ref.shape[1]
    pad = jnp.zeros((LANES - GK_LANE - GLA_RANK, k), BF16)
    w = jnp.concatenate([_bf(wba_ref[0]), _bf(wgk_ref[0]), pad], axis=0)
    o_ref[...] = _bdot_nt(a_ref[...], w)


def small_proj(a, w_t, l, tm):
    m, k = a.shape
    return pl.pallas_call(
        _small_kernel,
        grid=(m // tm, 1),
        in_specs=[pl.BlockSpec((tm, k), lambda i, j: (i, 0)),
                  _rows_spec(l, SRC_DN_B, 2 * DN_HEADS, k),
                  _rows_spec(l, SRC_GK, GLA_RANK, k)],
        out_specs=pl.BlockSpec((tm, LANES), lambda i, j: (i, 0)),
        out_shape=jax.ShapeDtypeStruct((m, LANES), jnp.float32),
        compiler_params=_cp(("parallel", "arbitrary")),
        name="small_proj",
    )(a, w_t, w_t)


def _swiglu_kernel(a_ref, wg_ref, wu_ref, o_ref, *, tn):
    w = jnp.concatenate([_bf(wg_ref[...]), _bf(wu_ref[...])], axis=1)
    r = _bdot(a_ref[...], w)
    g = r[:, :tn]
    u = r[:, tn:]
    o_ref[...] = (g * _sigmoid(g) * u).astype(o_ref.dtype)


def swiglu_in(a, w_in, l, tm, tn):
    m, k = a.shape
    nf = w_in.shape[2] // 2
    nj = nf // tn
    return pl.pallas_call(
        functools.partial(_swiglu_kernel, tn=tn),
        grid=(m // tm, nj),
        in_specs=[pl.BlockSpec((tm, k), lambda i, j: (i, 0)),
                  pl.BlockSpec((None, k, tn), lambda i, j: (l, 0, j)),
                  pl.BlockSpec((None, k, tn), lambda i, j: (l, 0, nj + j))],
        out_specs=pl.BlockSpec((tm, tn), lambda i, j: (i, j)),
        out_shape=jax.ShapeDtypeStruct((m, nf), BF16),
        compiler_params=_cp(("parallel", "arbitrary"), WIDE_VMEM),
        name="swiglu_in",
    )(a, w_in, w_in)


def _mm_res_kernel(a_ref, w_ref, r_ref, o_ref, *, scale, ksplit):
    k = a_ref.shape[1]
    kc = k // ksplit
    acc = _bdot(a_ref[:, :kc], w_ref[:kc, :])
    for s in range(1, ksplit):
        acc = acc + _bdot(a_ref[:, s * kc:(s + 1) * kc], w_ref[s * kc:(s + 1) * kc, :])
    o_ref[...] = r_ref[...] + scale * acc


def matmul_residual(a, w, l, res, scale, tm, tn, ksplit=1, vmem=VMEM_LIMIT):
    m, k = a.shape
    n = w.shape[2]
    return pl.pallas_call(
        functools.partial(_mm_res_kernel, scale=scale, ksplit=ksplit),
        grid=(m // tm, n // tn),
        in_specs=[_resident((tm, k), lambda i, j: (i, 0)),
                  pl.BlockSpec((None, k, tn), lambda i, j: (l, 0, j)),
                  pl.BlockSpec((tm, tn), lambda i, j: (i, j))],
        out_specs=pl.BlockSpec((tm, tn), lambda i, j: (i, j)),
        out_shape=jax.ShapeDtypeStruct((m, n), jnp.float32),
        compiler_params=_cp(("parallel", "arbitrary"), vmem),
        name="matmul_residual",
    )(a, w, res)


def _merge_kernel(a0_ref, a1_ref, a2_ref, w0_ref, w1_ref, w2_ref,
                  g0_ref, g1_ref, g2_ref, o_ref):
    f32 = jnp.float32
    y = _sigmoid(g0_ref[...].astype(f32)) * _bdot(a0_ref[...], w0_ref[...])
    y = y + _sigmoid(g1_ref[...].astype(f32)) * _bdot(a1_ref[...], w1_ref[...])
    y = y + _sigmoid(g2_ref[...].astype(f32)) * _bdot(a2_ref[...], w2_ref[...])
    o_ref[...] = y.astype(o_ref.dtype)


def branch_merge(o_dn, o_hg, o_gl, w_branch, l, gates, tm, tn):
    m = o_dn.shape[0]
    n = D_MODEL
    per_branch = D_MODEL // tn
    hg_blk = DN_V // HG_V
    gl_blk = (DN_V + HG_V) // GLA_V

    def gate_spec(b):
        return pl.BlockSpec((tm, tn), lambda i, j: (i, b * per_branch + j))

    return pl.pallas_call(
        _merge_kernel,
        grid=(m // tm, n // tn),
        in_specs=[_resident((tm, DN_V), lambda i, j: (i, 0)),
                  _resident((tm, HG_V), lambda i, j: (i, 0)),
                  _resident((tm, GLA_V), lambda i, j: (i, 0)),
                  pl.BlockSpec((None, DN_V, tn), lambda i, j: (l, 0, j)),
                  pl.BlockSpec((None, HG_V, tn), lambda i, j: (l, hg_blk, j)),
                  pl.BlockSpec((None, GLA_V, tn), lambda i, j: (l, gl_blk, j)),
                  gate_spec(0), gate_spec(1), gate_spec(2)],
        out_specs=pl.BlockSpec((tm, tn), lambda i, j: (i, j)),
        out_shape=jax.ShapeDtypeStruct((m, n), BF16),
        compiler_params=_cp(("parallel", "arbitrary")),
        name="branch_merge",
    )(o_dn, o_hg, o_gl, w_branch, w_branch, w_branch, gates, gates, gates)


def _iota2(shape, dim):
    return lax.broadcasted_iota(jnp.int32, shape, dim)


def _chunk_cumsum_matrix(tb, lower):
    r = _iota2((tb, tb), 0)
    c = _iota2((tb, tb), 1)
    same = (r // CHUNK) == (c // CHUNK)
    tri = (c <= r) if lower else (r <= c)
    return jnp.where(same & tri, 1.0, 0.0).astype(jnp.float32)


def _head_out(o, nw, gate):
    y = o * lax.rsqrt(jnp.mean(o * o, axis=-1, keepdims=True) + EPS) * nw
    return y * (gate * _sigmoid(gate))


def _chunks(x, nc):
    return [x[c * CHUNK:(c + 1) * CHUNK] for c in range(nc)]


def _unit_lower_inverse(a_strict):
    r = _iota2((CHUNK, CHUNK), 0)
    c = _iota2((CHUNK, CHUNK), 1)
    eye = jnp.where(r == c, 1.0, 0.0).astype(jnp.float32)
    blk = (r // SUB) == (c // SUB)
    d = jnp.where(blk, a_strict, 0.0)
    e = a_strict - d
    d2 = _bmm(d, d)
    d4 = _bmm(d2, d2)
    d8 = _bmm(d4, d4)
    p = eye - d
    p = p + _bmm(p, d2)
    p = p + _bmm(p, d4)
    p = p + _bmm(p, d8)
    n = _bmm(p, e)
    n2 = _bmm(n, n)
    q = eye - n + n2 - _bmm(n, n2)
    return q, p


def _dn_kernel(q_ref, k_ref, v_ref, z_ref, sm_ref, cq_ref, ck_ref, cv_ref,
               alog_ref, dtb_ref, nw_ref, o_ref, s_ref, carry_ref, *, tb):
    g = pl.program_id(0)
    t = pl.program_id(1)

    @pl.when(t == 0)
    def _():
        s_ref[...] = jnp.zeros_like(s_ref)
        carry_ref[...] = jnp.zeros_like(carry_ref)

    nc = tb // CHUNK
    hb = DN_HB

    def conv_silu(x_ref, cw_ref, idx):
        x = x_ref[...]
        ext = jnp.concatenate([carry_ref[idx], x], axis=0)
        w = cw_ref[...]
        y = x * w[DN_CONV - 1:DN_CONV]
        for s in range(1, DN_CONV):
            y = y + pltpu.roll(ext, s, 0)[8:] * w[DN_CONV - 1 - s:DN_CONV - s]
        carry_ref[idx] = x[tb - 8:]
        return y * _sigmoid(y)

    yq = conv_silu(q_ref, cq_ref, 0)
    yk = conv_silu(k_ref, ck_ref, 1)
    yv = conv_silu(v_ref, cv_ref, 2)

    sm = sm_ref[...]
    sm = jnp.where(g == 0, sm, pltpu.roll(sm, LANES - hb, 1))
    lane = _iota2((1, LANES), 1)
    is_alpha = (lane >= ALPHA_LANE) & (lane < ALPHA_LANE + hb)
    a_coef = jnp.where(is_alpha, jnp.exp(alog_ref[0]), 0.0)
    beta_slab = _sigmoid(sm)
    la_slab = -a_coef * _softplus(sm + dtb_ref[0])
    bcum_slab = _dot(_chunk_cumsum_matrix(tb, True), la_slab, HI)
    la_t = la_slab.T[ALPHA_LANE:ALPHA_LANE + 8]
    bcum_t = _dot(la_t, _chunk_cumsum_matrix(tb, False), HI)

    def stack_heads(per_head):
        return jnp.stack([per_head[h][c] for c in range(nc) for h in range(hb)], axis=0)

    qs, ks, vs, betas, bcs = [], [], [], [], []
    for h in range(hb):
        hs = slice(h * DN_DK, (h + 1) * DN_DK)
        qh, kh = yq[:, hs], yk[:, hs]
        qn = qh * lax.rsqrt(jnp.sum(qh * qh, axis=-1, keepdims=True) + EPS) * (DN_DK ** -0.5)
        kn = kh * lax.rsqrt(jnp.sum(kh * kh, axis=-1, keepdims=True) + EPS)
        qs.append(_chunks(qn, nc))
        ks.append(_chunks(kn, nc))
        vs.append(_chunks(yv[:, hs], nc))
        betas.append(_chunks(beta_slab[:, h:h + 1], nc))
        bcs.append(_chunks(bcum_slab[:, ALPHA_LANE + h:ALPHA_LANE + h + 1], nc))
    q3, k3, v3 = stack_heads(qs), stack_heads(ks), stack_heads(vs)
    beta3, bc3 = stack_heads(betas), stack_heads(bcs)
    br3 = jnp.stack([bcum_t[h:h + 1, c * CHUNK:(c + 1) * CHUNK]
                     for c in range(nc) for h in range(hb)], axis=0)

    r64 = _iota2((CHUNK, CHUNK), 0)
    c64 = _iota2((CHUNK, CHUNK), 1)
    causal = c64 <= r64
    strict = c64 < r64
    ebc3 = jnp.exp(bc3)
    kb3 = k3 * beta3
    decay3 = jnp.where(causal, jnp.exp(jnp.where(causal, bc3 - br3, 0.0)), 0.0)
    kq3 = _bmm_nt(jnp.concatenate([kb3, q3], axis=1), k3)
    kk3 = kq3[:, :CHUNK] * decay3
    qk3 = kq3[:, CHUNK:] * decay3
    qmat, pmat = _unit_lower_inverse(jnp.where(strict, kk3, 0.0))
    rhs3 = jnp.concatenate([v3 * beta3, kb3 * ebc3], axis=2)
    sol3 = _bmm(qmat, _bmm(pmat, rhs3))
    u3 = sol3[:, :, :DN_DV]
    wq3 = jnp.concatenate([sol3[:, :, DN_DV:], q3 * ebc3], axis=1)
    b_last3 = bc3[:, CHUNK - 1:CHUNK, :]
    kd3 = k3 * jnp.exp(b_last3 - bc3)
    kdt3 = jnp.stack([kd3[b].T for b in range(nc * hb)], axis=0)
    a_last3 = jnp.exp(b_last3)

    nw = nw_ref[...]
    z_all = z_ref[...]
    state = s_ref[...]
    for c in range(nc):
        bs = slice(c * hb, (c + 1) * hb)
        rs = slice(c * CHUNK, (c + 1) * CHUNK)
        ws = _bmm(wq3[bs], state)
        v_new = u3[bs] - ws[:, :CHUNK]
        o = ws[:, CHUNK:] + _bmm(qk3[bs], v_new)
        state = state * a_last3[bs] + _bmm(kdt3[bs], v_new)
        for h in range(hb):
            hs = slice(h * DN_DK, (h + 1) * DN_DK)
            o_ref[rs, hs] = _head_out(o[h], nw, z_all[rs, hs]).astype(o_ref.dtype)
    s_ref[...] = state


def deltanet(proj_a, small, dn_conv, l, a_log_pad, dtb_pad, dn_norm, tb):
    t = proj_a.shape[0]
    w = DN_HB * DN_DK
    qb, kb_, vb, zb = (A_DN_Q // w, A_DN_K // w, A_DN_V // w, A_DN_Z // w)

    def pspec(base):
        return pl.BlockSpec((tb, w), lambda g, i: (i, base + g))

    def cspec(base):
        return pl.BlockSpec((None, DN_CONV, w), lambda g, i: (l, 0, base + g))

    pad_spec = pl.BlockSpec((1, 1, LANES), lambda g, i: (g, 0, 0))
    assert DN_GROUPS == 2, "the kernel selects its group's lanes with one rotate"
    return pl.pallas_call(
        functools.partial(_dn_kernel, tb=tb),
        grid=(DN_GROUPS, t // tb),
        in_specs=[pspec(qb), pspec(kb_), pspec(vb), pspec(zb),
                  pl.BlockSpec((tb, LANES), lambda g, i: (i, 0)),
                  cspec(qb), cspec(kb_), cspec(vb),
                  pad_spec, pad_spec,
                  pl.BlockSpec((1, DN_DV), lambda g, i: (0, 0))],
        out_specs=pl.BlockSpec((tb, w), lambda g, i: (i, g)),
        out_shape=jax.ShapeDtypeStruct((t, DN_V), BF16),
        scratch_shapes=[pltpu.VMEM((DN_HB, DN_DK, DN_DV), jnp.float32),
                        pltpu.VMEM((3, 8, w), jnp.float32)],
        compiler_params=_cp(("parallel", "arbitrary")),
        name="deltanet",
    )(proj_a, proj_a, proj_a, proj_a, small, dn_conv, dn_conv, dn_conv,
      a_log_pad, dtb_pad, dn_norm.reshape(1, DN_DV))


def _gla_core(q, k, v, g, gate, nw, s_ref, o_ref, tb, hb, dk, dv):
    sub = GLA_SUB
    nsub = tb // sub
    nc = tb // CHUNK
    per = CHUNK // sub
    wk = hb * dk
    b = _dot(_chunk_cumsum_matrix(tb, True), g, HI)

    b3 = b.reshape(nsub, sub, wk)
    last = b3[:, sub - 1:sub, :]
    prev = jnp.concatenate([jnp.zeros((1, 1, wk), jnp.float32), last[:nsub - 1]], axis=0)
    blk_id = _iota2((nsub, 1, wk), 0)
    base3 = jnp.where(blk_id % per == 0, 0.0, prev)
    base = jnp.broadcast_to(base3, (nsub, sub, wk)).reshape(tb, wk)
    qt = q * jnp.exp(b - base)
    qe = q * jnp.exp(b)

    q3 = q.reshape(nsub, sub, wk)
    k3 = k.reshape(nsub, sub, wk)
    row = _iota2((tb, CHUNK), 0)
    col = _iota2((tb, CHUNK), 1)
    same_blk = ((row % CHUNK) // sub) == (col // sub)
    diag = [jnp.zeros((tb, CHUNK), jnp.float32) for _ in range(hb)]
    for j in range(sub):
        bj = b3[:, j:j + 1, :]
        kj = k3[:, j:j + 1, :]
        e = jnp.exp(jnp.minimum(b3 - bj, 0.0))
        prod = (q3 * kj * e).reshape(tb, wk)
        sel = same_blk & ((col % sub) == j) & ((row % sub) >= j)
        for h in range(hb):
            colv = jnp.sum(prod[:, h * dk:(h + 1) * dk], axis=-1, keepdims=True)
            diag[h] = jnp.where(sel, colv, diag[h])

    zeros = lambda r: jnp.zeros((r, dk), jnp.float32)
    lhs_l, rhs_l, tr_l, v_l, qe_l, diag_l = [], [], [], [], [], []
    for c in range(nc):
        rs = slice(c * CHUNK, (c + 1) * CHUNK)
        for h in range(hb):
            ks_ = slice(h * dk, (h + 1) * dk)
            b_c = b[rs, ks_]
            k_c = k[rs, ks_]
            qt_c = qt[rs, ks_]
            lhs, rhs = [], []
            for i in range(1, per):
                lo, hi = i * sub, (i + 1) * sub
                n_i = b_c[lo - 1:lo]
                kt = k_c[:lo] * jnp.exp(n_i - b_c[:lo])
                rhs.append(jnp.concatenate([kt, zeros(CHUNK - lo)], axis=0))
                blk = [zeros(lo), qt_c[lo:hi]] + ([zeros(CHUNK - hi)] if hi < CHUNK else [])
                lhs.append(jnp.concatenate(blk, axis=0))
            lhs_l.append(jnp.concatenate(lhs, axis=1))
            rhs_l.append(jnp.concatenate(rhs, axis=1))
            b_last = b_c[CHUNK - 1:CHUNK]
            kg = k_c * jnp.exp(b_last - b_c)
            e_rows = jnp.broadcast_to(jnp.exp(b_last), (CHUNK, dk))
            tr_l.append(jnp.concatenate([kg, e_rows], axis=0).T)
            v_l.append(v[rs, h * dv:(h + 1) * dv])
            qe_l.append(qe[rs, ks_])
            diag_l.append(diag[h][rs])
    off = _bmm_nt(jnp.stack(lhs_l, axis=0), jnp.stack(rhs_l, axis=0))
    scores = jnp.stack(diag_l, axis=0) + off
    tr3 = jnp.stack(tr_l, axis=0)
    v3 = jnp.stack(v_l, axis=0)
    upd = _bmm(tr3[:, :, :CHUNK], v3)
    state = s_ref[...]
    states = []
    for c in range(nc):
        bs = slice(c * hb, (c + 1) * hb)
        states.append(state)
        state = state * tr3[bs, :, CHUNK:CHUNK + 1] + upd[bs]
    s_ref[...] = state
    o3 = _bmm(jnp.concatenate([jnp.stack(qe_l, axis=0), scores], axis=2),
              jnp.concatenate([jnp.concatenate(states, axis=0), v3], axis=1))
    for c in range(nc):
        rs = slice(c * CHUNK, (c + 1) * CHUNK)
        for h in range(hb):
            vs_ = slice(h * dv, (h + 1) * dv)
            o_ref[rs, vs_] = _head_out(o3[c * hb + h], nw, gate[rs, vs_]).astype(o_ref.dtype)


def _hg_kernel(q_ref, f_ref, i_ref, g_ref, lb_ref, nw_ref, o_ref, s_ref, *, tb):
    @pl.when(pl.program_id(1) == 0)
    def _():
        s_ref[...] = jnp.zeros_like(s_ref)

    lb = lb_ref[...]
    xf = f_ref[...]
    a = jnp.log(lb)
    cc = jnp.log1p(-lb) + _log_sigmoid(xf)
    log_f = jnp.maximum(a, cc) + jnp.log(1.0 + jnp.exp(-jnp.abs(a - cc)))
    k_in = (1.0 - lb) * _sigmoid(-xf)
    q = q_ref[...] * (HG_DK ** -0.5)
    _gla_core(q, k_in, i_ref[...], log_f, g_ref[...], nw_ref[...], s_ref, o_ref, tb,
              HG_HB, HG_DK, HG_DV)


def hgrn2(proj_b, lb, hg_norm, tb):
    t = proj_b.shape[0]
    w = HG_HB * HG_DK

    def pspec(off):
        return pl.BlockSpec((tb, w), lambda g, i: (i, off // w + g))

    return pl.pallas_call(
        functools.partial(_hg_kernel, tb=tb),
        grid=(HG_HEADS // HG_HB, t // tb),
        in_specs=[pspec(B_HG_Q), pspec(B_HG_F), pspec(B_HG_I), pspec(B_HG_G),
                  pl.BlockSpec((1, w), lambda g, i: (0, g)),
                  pl.BlockSpec((1, HG_DV), lambda g, i: (0, 0))],
        out_specs=pl.BlockSpec((tb, w), lambda g, i: (i, g)),
        out_shape=jax.ShapeDtypeStruct((t, HG_V), BF16),
        scratch_shapes=[pltpu.VMEM((HG_HB, HG_DK, HG_DV), jnp.float32)],
        compiler_params=_cp(("parallel", "arbitrary")),
        name="hgrn2",
    )(proj_b, proj_b, proj_b, proj_b, lb.reshape(1, HG_K), hg_norm.reshape(1, HG_DV))


def _gl_kernel(q_ref, k_ref, v_ref, g_ref, sm_ref, w2_ref, bgk_ref, nw_ref, o_ref, s_ref, *, tb):
    @pl.when(pl.program_id(1) == 0)
    def _():
        s_ref[...] = jnp.zeros_like(s_ref)

    gk = _dot(sm_ref[...], w2_ref[...], HI) + bgk_ref[...]
    log_g = _log_sigmoid(gk) / GLA_GATE_NORM
    q = q_ref[...] * (GLA_DK ** -0.5)
    _gla_core(q, k_ref[...], v_ref[...], log_g, g_ref[...], nw_ref[...], s_ref, o_ref, tb,
              GLA_HB, GLA_DK, GLA_DV)


def gla(proj_b, small, w2_pad, b_gk, gla_norm, tb):
    t = proj_b.shape[0]
    wk = GLA_HB * GLA_DK
    wv = GLA_HB * GLA_DV
    return pl.pallas_call(
        functools.partial(_gl_kernel, tb=tb),
        grid=(GLA_HEADS // GLA_HB, t // tb),
        in_specs=[pl.BlockSpec((tb, wk), lambda g, i: (i, B_GL_Q // wk + g)),
                  pl.BlockSpec((tb, wk), lambda g, i: (i, B_GL_K // wk + g)),
                  pl.BlockSpec((tb, wv), lambda g, i: (i, B_GL_V // wv + g)),
                  pl.BlockSpec((tb, wv), lambda g, i: (i, B_GL_G // wv + g)),
                  pl.BlockSpec((tb, LANES), lambda g, i: (i, 0)),
                  pl.BlockSpec((LANES, wk), lambda g, i: (0, g)),
                  pl.BlockSpec((1, wk), lambda g, i: (0, g)),
                  pl.BlockSpec((1, GLA_DV), lambda g, i: (0, 0))],
        out_specs=pl.BlockSpec((tb, wv), lambda g, i: (i, g)),
        out_shape=jax.ShapeDtypeStruct((t, GLA_V), BF16),
        scratch_shapes=[pltpu.VMEM((GLA_HB, GLA_DK, GLA_DV), jnp.float32)],
        compiler_params=_cp(("parallel", "arbitrary")),
        name="gla",
    )(proj_b, proj_b, proj_b, proj_b, small, w2_pad, b_gk.reshape(1, GLA_K), gla_norm.reshape(1, GLA_DV))


def _pad_group_lanes(v):
    vg = v.reshape(DN_GROUPS, 1, DN_HB).astype(jnp.float32)
    return jnp.pad(vg, ((0, 0), (0, 0), (ALPHA_LANE, LANES - ALPHA_LANE - DN_HB)))


FFN_OUT_VMEM = 60 * 1024 * 1024


def _ffn(h, norm_w, w_in, w_out, l):
    u = rmsnorm(h, norm_w, BF16)
    a = swiglu_in(u, w_in, l, tm=2048, tn=256)
    return matmul_residual(a, w_out, l, h, 0.5, tm=1024, tn=256, ksplit=2, vmem=FFN_OUT_VMEM)


def _mixer(h, l, norm_w, w_t, dn_conv, dn_a_log, dn_dt_bias, dn_norm, lb, hg_norm,
           gla_w_gk2, gla_b_gk, gla_norm, w_branch, w_out):
    u = rmsnorm(h, norm_w, BF16)
    proj_a = matmul_nt(u, w_t, l, 0, SRC_DN_B, jnp.float32, tm=2048, tn=512)
    proj_b = matmul_nt(u, w_t, l, SRC_HG, B_COLS, jnp.float32, tm=2048, tn=512)
    gates = matmul_nt(u, w_t, l, SRC_GATES, N_BRANCH * D_MODEL, BF16, tm=2048, tn=512)
    small = small_proj(u, w_t, l, tm=1024)
    o_dn = deltanet(proj_a, small, dn_conv, l, _pad_group_lanes(dn_a_log), _pad_group_lanes(dn_dt_bias),
                    dn_norm, DN_TB)
    o_hg = hgrn2(proj_b, lb, hg_norm, GLA_TB)
    w2_pad = jnp.pad(gla_w_gk2.astype(jnp.float32), ((GK_LANE, LANES - GK_LANE - GLA_RANK), (0, 0)))
    o_gl = gla(proj_b, small, w2_pad, gla_b_gk, gla_norm, GLA_TB)
    y = branch_merge(o_dn, o_hg, o_gl, w_branch, l, gates, tm=2048, tn=256)
    return matmul_residual(y, w_out, l, h, 1.0, tm=2048, tn=256)


def kernel(x, norm_ffn1, ffn1_w_in, ffn1_w_out, norm_mix, w_in, dn_conv, dn_a_log, dn_dt_bias,
           dn_norm, hg_lower_bounds, hg_norm, gla_w_gk2, gla_b_gk, gla_norm, w_branch, w_out,
           norm_ffn2, ffn2_w_in, ffn2_w_out, norm_final):
    bsz, t, d = x.shape
    assert bsz == 1, "the chunked recurrences run over one sequence"
    lb_all = jnp.cumsum(jax.nn.softmax(hg_lower_bounds.astype(jnp.float32), axis=0), axis=0)
    lb_all = lb_all - lb_all[0]
    w_t = jnp.swapaxes(w_in, 1, 2)
    h = x.reshape(t, d)
    for l in range(DEPTH):
        h = _ffn(h, norm_ffn1[l], ffn1_w_in, ffn1_w_out, l)
        h = _mixer(h, l, norm_mix[l], w_t, dn_conv, dn_a_log[l], dn_dt_bias[l],
                   dn_norm[l], lb_all[l], hg_norm[l], gla_w_gk2[l], gla_b_gk[l], gla_norm[l],
                   w_branch, w_out)
        h = _ffn(h, norm_ffn2[l], ffn2_w_in, ffn2_w_out, l)
    return rmsnorm(h, norm_final, x.dtype).reshape(bsz, t, d)
```
